```python
import math
import jax, jax.numpy as jnp
from jax import lax
import numpy as np

D_MODEL = 2048
BATCH = 1
SEQ = 8192
DEPTH = 4

GRID_W = 64
HEAD_DIM = 128
NA_HEADS = 8
NA_ROWS = 8
NA_COLS = 16
WG_HEADS = 8
WG_KV_HEADS = 2
WG_WINDOW = 128
WG_BLOCK = 128
SC_WIDTH = 1024
SC_KSIZE = 3
NA_WIDTH = NA_HEADS * HEAD_DIM
WG_Q_WIDTH = WG_HEADS * HEAD_DIM
WG_KV_WIDTH = WG_KV_HEADS * HEAD_DIM
BRANCH_WIDTH = 1024
N_BRANCH = 3
IN_SPLITS = (NA_WIDTH, NA_WIDTH, NA_WIDTH,
             WG_Q_WIDTH, WG_KV_WIDTH, WG_KV_WIDTH,
             SC_WIDTH, SC_WIDTH, SC_WIDTH,
             N_BRANCH * D_MODEL)
IN_WIDTH = sum(IN_SPLITS)
N_EXPERTS = 16
EC_CAPACITY_FACTOR = 2
D_FF = 1536
DEEPNORM_ALPHA = (2 * DEPTH) ** 0.25
DEEPNORM_BETA = (8 * DEPTH) ** -0.25
LN_EPS = 1e-5

kernel_name = 'hybrid_natten_swa_shortconv_ec_moe_deepnorm'


def layer_norm(x, g, b):
    xf = x.astype(jnp.float32)
    mu = xf.mean(-1, keepdims=True)
    var = jnp.square(xf - mu).mean(-1, keepdims=True)
    y = (xf - mu) * lax.rsqrt(var + LN_EPS) * g.astype(jnp.float32) + b.astype(jnp.float32)
    return y.astype(x.dtype)


def neighbourhood_attention(q, k, v, rpb):
    b, s, h, dh = q.shape
    rows = s // GRID_W
    kr = min(NA_ROWS, rows)
    kc = NA_COLS
    qg = q.reshape(b, rows, GRID_W, h, dh)
    kg = k.reshape(b, rows, GRID_W, h, dh)
    vg = v.reshape(b, rows, GRID_W, h, dh)
    col = jnp.arange(GRID_W)
    col_idx = jnp.clip(col - kc // 2, 0, GRID_W - kc)[:, None] + jnp.arange(kc)[None, :]
    col_off = col_idx - col[:, None] + (NA_COLS - 1)
    row = jnp.arange(rows)
    row_start = jnp.clip(row - kr // 2, 0, rows - kr)
    scale = HEAD_DIM ** -0.5

    def row_block(args):
        q_r, r, rs = args
        k_rows = lax.dynamic_slice_in_dim(kg, rs, kr, axis=1)
        v_rows = lax.dynamic_slice_in_dim(vg, rs, kr, axis=1)
        k_win = k_rows[:, :, col_idx]
        v_win = v_rows[:, :, col_idx]
        row_off = rs + jnp.arange(kr) - r + (NA_ROWS - 1)
        bias = rpb[:, row_off[:, None, None], col_off[None]]
        bias = jnp.transpose(bias, (0, 2, 1, 3)).astype(jnp.float32)
        logits = jnp.einsum('bqhd,biqjhd->bhqij', q_r, k_win).astype(jnp.float32) * scale + bias[None]
        p = jax.nn.softmax(logits.reshape(b, h, GRID_W, kr * kc), axis=-1)
        p = p.reshape(b, h, GRID_W, kr, kc).astype(v.dtype)
        return jnp.einsum('bhqij,biqjhd->bqhd', p, v_win)

    out = lax.map(row_block, (jnp.moveaxis(qg, 1, 0), row, row_start))
    return jnp.moveaxis(out, 0, 1).reshape(b, s, h * dh)


def windowed_gqa(q, k, v, sink):
    b, s, h, dh = q.shape
    hkv = k.shape[2]
    g = h // hkv
    nb = s // WG_BLOCK
    qb = q.reshape(b, nb, WG_BLOCK, hkv, g, dh)

    def band(t):
        tp = jnp.pad(t, ((0, 0), (WG_BLOCK, WG_BLOCK), (0, 0), (0, 0)))
        tp = tp.reshape(b, nb + 2, WG_BLOCK, hkv, dh)
        return jnp.concatenate([tp[:, :-2], tp[:, 1:-1], tp[:, 2:]], axis=2)

    kb, vb = band(k), band(v)
    q_pos = jnp.arange(WG_BLOCK)
    k_pos = jnp.arange(3 * WG_BLOCK) - WG_BLOCK
    dist = jnp.abs(k_pos[None, :] - q_pos[:, None])
    k_abs = jnp.arange(nb)[:, None, None] * WG_BLOCK + k_pos[None, None, :]
    valid = (dist[None] <= WG_WINDOW) & (k_abs >= 0) & (k_abs < s)
    slopes = 2.0 ** (-8.0 * jnp.arange(1, h + 1, dtype=jnp.float32) / h)
    scale = HEAD_DIM ** -0.5
    logits = jnp.einsum('bnqkgd,bnskd->bnkgqs', qb, kb).astype(jnp.float32) * scale
    logits = logits - slopes.reshape(hkv, g)[:, :, None, None] * dist.astype(jnp.float32)
    logits = jnp.where(valid[None, :, None, None], logits, -jnp.inf)
    sink_l = sink.astype(jnp.float32).reshape(hkv, g)[:, :, None, None]
    m = jnp.maximum(logits.max(-1, keepdims=True), sink_l)
    e = jnp.exp(logits - m)
    p = e / (e.sum(-1, keepdims=True) + jnp.exp(sink_l - m))
    out = jnp.einsum('bnkgqs,bnskd->bnqkgd', p.astype(v.dtype), vb)
    return out.reshape(b, s, h * dh)


def short_conv_mixer(bg, cg, hx, conv_w):
    u = cg * hx
    u_prev = jnp.pad(u, ((0, 0), (1, 0), (0, 0)))[:, :-1]
    u_next = jnp.pad(u, ((0, 0), (0, 1), (0, 0)))[:, 1:]
    return bg * (conv_w[0] * u_prev + conv_w[1] * u + conv_w[2] * u_next)


def expert_choice_ffn(x, w_router, w_gate, w_up, w_down):
    b, s, d = x.shape
    cap = EC_CAPACITY_FACTOR * s // N_EXPERTS
    aff = jax.nn.softmax(jnp.einsum('bsd,de->bse', x, w_router).astype(jnp.float32), axis=-1)
    gates, idx = lax.top_k(jnp.swapaxes(aff, 1, 2), cap)
    xe = jax.vmap(lambda xb, ib: xb[ib])(x, idx)
    hid = jax.nn.silu(jnp.einsum('becd,edf->becf', xe, w_gate)) * jnp.einsum('becd,edf->becf', xe, w_up)
    ye = jnp.einsum('becf,efd->becd', hid, w_down) * gates[..., None].astype(x.dtype)
    return jax.vmap(lambda ib, yb: jnp.zeros((s, d), yb.dtype).at[ib.reshape(-1)].add(yb.reshape(-1, d)))(idx, ye)


def setup_inputs(seed: int = 0) -> dict:
    key = jax.random.key(seed)
    ks = jax.random.split(key, 14)
    f32 = jnp.float32
    x = jax.random.normal(ks[0], (BATCH, SEQ, D_MODEL), f32)
    w_in = jax.random.normal(ks[1], (DEPTH, D_MODEL, IN_WIDTH), f32) * D_MODEL ** -0.5
    b_gate = jax.random.normal(ks[2], (DEPTH, N_BRANCH * D_MODEL), f32) * 0.02
    rpb = jax.random.normal(ks[3], (DEPTH, NA_HEADS, 2 * NA_ROWS - 1, 2 * NA_COLS - 1), f32) * 0.1
    sink = jax.random.normal(ks[4], (DEPTH, WG_HEADS), f32)
    conv_w = jax.random.normal(ks[5], (DEPTH, SC_KSIZE, SC_WIDTH), f32) * SC_KSIZE ** -0.5
    w_branch = jax.random.normal(ks[6], (DEPTH, N_BRANCH, BRANCH_WIDTH, D_MODEL), f32) * (BRANCH_WIDTH ** -0.5 * DEEPNORM_BETA)
    w_out = jax.random.normal(ks[7], (DEPTH, D_MODEL, D_MODEL), f32) * (D_MODEL ** -0.5 * DEEPNORM_BETA)
    ln_g = 1.0 + 0.02 * jax.random.normal(ks[8], (DEPTH, 2, D_MODEL), f32)
    ln_b = 0.02 * jax.random.normal(ks[9], (DEPTH, 2, D_MODEL), f32)
    w_router = jax.random.normal(ks[10], (DEPTH, D_MODEL, N_EXPERTS), f32) * D_MODEL ** -0.5
    w_gate = jax.random.normal(ks[11], (DEPTH, N_EXPERTS, D_MODEL, D_FF), f32) * D_MODEL ** -0.5
    w_up = jax.random.normal(ks[12], (DEPTH, N_EXPERTS, D_MODEL, D_FF), f32) * D_MODEL ** -0.5
    w_down = jax.random.normal(ks[13], (DEPTH, N_EXPERTS, D_FF, D_MODEL), f32) * (D_FF ** -0.5 * DEEPNORM_BETA)
    return {'x': x, 'w_in': w_in, 'b_gate': b_gate, 'rpb': rpb, 'sink': sink, 'conv_w': conv_w,
            'w_branch': w_branch, 'w_out': w_out, 'ln_g': ln_g, 'ln_b': ln_b, 'w_router': w_router,
            'w_gate': w_gate, 'w_up': w_up, 'w_down': w_down}


def reference(x, w_in, b_gate, rpb, sink, conv_w, w_branch, w_out, ln_g, ln_b, w_router, w_gate, w_up, w_down):
    b, s, d = x.shape
    offsets = tuple(int(o) for o in np.cumsum(IN_SPLITS)[:-1])
    for l in range(DEPTH):
        proj = jnp.einsum('bsd,dn->bsn', x, w_in[l])
        (qa, ka, va, qb, kb, vb, bg, cg, hc, gate_logits) = jnp.split(proj, offsets, axis=-1)
        ya = neighbourhood_attention(qa.reshape(b, s, NA_HEADS, HEAD_DIM), ka.reshape(b, s, NA_HEADS, HEAD_DIM),
                                     va.reshape(b, s, NA_HEADS, HEAD_DIM), rpb[l])
        yb = windowed_gqa(qb.reshape(b, s, WG_HEADS, HEAD_DIM), kb.reshape(b, s, WG_KV_HEADS, HEAD_DIM),
                          vb.reshape(b, s, WG_KV_HEADS, HEAD_DIM), sink[l])
        yc = short_conv_mixer(bg, cg, hc, conv_w[l])
        branches = jnp.stack([ya, yb, yc], axis=2)
        branches = jnp.einsum('bsnc,ncd->bsnd', branches, w_branch[l])
        gates = jax.nn.sigmoid(gate_logits.reshape(b, s, N_BRANCH, d) + b_gate[l].reshape(N_BRANCH, d))
        merged = jnp.sum(gates * branches, axis=2)
        mix = jnp.einsum('bsd,de->bse', merged, w_out[l])
        x = layer_norm(DEEPNORM_ALPHA * x + mix, ln_g[l, 0], ln_b[l, 0])
        moe = expert_choice_ffn(x, w_router[l], w_gate[l], w_up[l], w_down[l])
        x = layer_norm(DEEPNORM_ALPHA * x + moe, ln_g[l, 1], ln_b[l, 1])
    return x
```

```python
import functools

import jax
import jax.numpy as jnp
from jax import lax
from jax.experimental import pallas as pl
from jax.experimental.pallas import tpu as pltpu

F32 = jnp.float32
BF16 = jnp.bfloat16

D_MODEL = 2048
SEQ = 8192
DEPTH = 4
GRID_W = 64
ROWS = SEQ // GRID_W
HEAD_DIM = 128
NA_HEADS = 8
NA_ROWS = 8
NA_COLS = 16
WG_HEADS = 8
WG_KV_HEADS = 2
WG_GROUP = WG_HEADS // WG_KV_HEADS
WG_WINDOW = 128
WG_BLOCK = 128
SC_WIDTH = 1024
BRANCH_WIDTH = 1024
N_BRANCH = 3
N_EXPERTS = 16
CAPACITY = 2 * SEQ // N_EXPERTS
D_FF = 1536
ALPHA = (2 * DEPTH) ** 0.25
LN_EPS = 1e-5
ATTN_SCALE = HEAD_DIM ** -0.5
MASKED = -1e30

QKV_WIDTH = 3 * NA_HEADS * HEAD_DIM + (WG_HEADS + 2 * WG_KV_HEADS) * HEAD_DIM
REST_WIDTH = 3 * SC_WIDTH + N_BRANCH * D_MODEL
NA_Q_BLK, NA_K_BLK, NA_V_BLK = 0, NA_HEADS, 2 * NA_HEADS
WG_Q_BLK = 3 * NA_HEADS * HEAD_DIM // (WG_GROUP * HEAD_DIM)
WG_K_BLK = (3 * NA_HEADS + WG_HEADS) * HEAD_DIM // HEAD_DIM
WG_V_BLK = WG_K_BLK + WG_KV_HEADS
GATE_COL = 3 * SC_WIDTH

MIB = 1024 * 1024


def _params(vmem_mib, n_axes):
    return pltpu.CompilerParams(dimension_semantics=("arbitrary",) * n_axes,
                                vmem_limit_bytes=vmem_mib * MIB)


def _cast_body(x_ref, o_ref):
    o_ref[...] = x_ref[...].astype(BF16)


def _cast_rows(x, tm):
    m, n = x.shape
    return pl.pallas_call(
        _cast_body,
        grid=(m // tm,),
        in_specs=[pl.BlockSpec((tm, n), lambda i: (i, 0))],
        out_specs=pl.BlockSpec((tm, n), lambda i: (i, 0)),
        out_shape=jax.ShapeDtypeStruct((m, n), BF16),
        compiler_params=_params(32, 1),
        name="cast_rows",
    )(x)


def _cast_layer_body(l_ref, x_ref, o_ref):
    o_ref[...] = x_ref[...].astype(BF16)


def _cast_layer(l, w, tm):
    _, m, n = w.shape
    return pl.pallas_call(
        _cast_layer_body,
        grid_spec=pltpu.PrefetchScalarGridSpec(
            num_scalar_prefetch=1,
            grid=(m // tm,),
            in_specs=[pl.BlockSpec((None, tm, n), lambda i, l_ref: (l_ref[0], i, 0))],
            out_specs=pl.BlockSpec((tm, n), lambda i, l_ref: (i, 0)),
        ),
        out_shape=jax.ShapeDtypeStruct((m, n), BF16),
        compiler_params=_params(32, 1),
        name="cast_layer",
    )(l, w)


PROJ_TM = 1024
PROJ_TN = 768
CAST_ROWS = 256


def _cast_into(w_ref, wb_ref, rows):
    def chunk(i, c):
        sl = pl.ds(pl.multiple_of(i * CAST_ROWS, CAST_ROWS), CAST_ROWS)
        wb_ref[sl, :] = w_ref[sl, :].astype(BF16)
        return c
    lax.fori_loop(0, rows // CAST_ROWS, chunk, 0)


def _proj_body(l_ref, x_ref, w_ref, o_ref, wb_ref):
    @pl.when(pl.program_id(1) == 0)
    def _():
        _cast_into(w_ref, wb_ref, D_MODEL)

    o_ref[...] = jnp.dot(x_ref[...], wb_ref[...], preferred_element_type=F32).astype(o_ref.dtype)


def _project(l, xb, w_in, col0, width, out_dtype):
    n_tiles, off = width // PROJ_TN, col0 // PROJ_TN
    return pl.pallas_call(
        _proj_body,
        grid_spec=pltpu.PrefetchScalarGridSpec(
            num_scalar_prefetch=1,
            grid=(n_tiles, SEQ // PROJ_TM),
            in_specs=[
                pl.BlockSpec((PROJ_TM, D_MODEL), lambda j, i, l_ref: (i, 0)),
                pl.BlockSpec((None, D_MODEL, PROJ_TN), lambda j, i, l_ref: (l_ref[0], 0, off + j)),
            ],
            out_specs=pl.BlockSpec((PROJ_TM, PROJ_TN), lambda j, i, l_ref: (i, j)),
            scratch_shapes=[pltpu.VMEM((D_MODEL, PROJ_TN), BF16)],
        ),
        out_shape=jax.ShapeDtypeStruct((SEQ, width), out_dtype),
        compiler_params=_params(48, 2),
        name="in_proj",
    )(l, xb, w_in)


NA_CHUNK_ROWS = 16
NA_KEYS = NA_ROWS * GRID_W


def _na_bias_tables(rpb):
    d = jnp.arange(NA_ROWS)[:, None]
    i = jnp.arange(NA_ROWS)[None, :]
    row_idx = i - d + (NA_ROWS - 1)
    c = jnp.arange(GRID_W)[:, None]
    cj = jnp.arange(GRID_W)[None, :]
    cs = jnp.clip(c - NA_COLS // 2, 0, GRID_W - NA_COLS)
    valid = (cj >= cs) & (cj < cs + NA_COLS)
    col_idx = jnp.clip(cj - c + (NA_COLS - 1), 0, 2 * NA_COLS - 2)
    t = rpb[:, :, row_idx[:, :, None, None], col_idx[None, None, :, :]]
    t = jnp.where(valid[None, None, None, None], t, MASKED)
    t = jnp.transpose(t, (0, 2, 1, 4, 3, 5))
    return t.reshape(DEPTH, NA_ROWS, NA_HEADS, GRID_W, NA_KEYS).astype(F32)


def _na_body(q_ref, k_ref, v_ref, bias_ref, o_ref):
    chunk = pl.program_id(1)

    def row(i, carry):
        r = chunk * NA_CHUNK_ROWS + i
        rs = jnp.clip(r - NA_ROWS // 2, 0, ROWS - NA_ROWS)
        d = r - rs
        qs = pl.ds(pl.multiple_of(i * GRID_W, GRID_W), GRID_W)
        ks = pl.ds(pl.multiple_of(rs * GRID_W, GRID_W), NA_KEYS)
        q = q_ref[qs, :]
        s = lax.dot_general(q, k_ref[ks, :], (((1,), (1,)), ((), ())), preferred_element_type=F32)
        s = s * ATTN_SCALE + bias_ref[d]
        m = jnp.max(s, axis=-1, keepdims=True)
        e = jnp.exp(s - m)
        den = jnp.sum(e, axis=-1, keepdims=True)
        o = jnp.dot(e.astype(BF16), v_ref[ks, :], preferred_element_type=F32) / den
        o_ref[qs, :] = o.astype(o_ref.dtype)
        return carry

    lax.fori_loop(0, NA_CHUNK_ROWS, row, 0, unroll=2)


def _neighbourhood_attention(qkv, bias):
    tq = NA_CHUNK_ROWS * GRID_W
    return pl.pallas_call(
        _na_body,
        grid=(NA_HEADS, ROWS // NA_CHUNK_ROWS),
        in_specs=[
            pl.BlockSpec((tq, HEAD_DIM), lambda h, c: (c, NA_Q_BLK + h)),
            pl.BlockSpec((SEQ, HEAD_DIM), lambda h, c: (0, NA_K_BLK + h)),
            pl.BlockSpec((SEQ, HEAD_DIM), lambda h, c: (0, NA_V_BLK + h)),
            pl.BlockSpec((NA_ROWS, None, GRID_W, NA_KEYS), lambda h, c: (0, h, 0, 0)),
        ],
        out_specs=pl.BlockSpec((tq, HEAD_DIM), lambda h, c: (c, h)),
        out_shape=jax.ShapeDtypeStruct((SEQ, NA_HEADS * HEAD_DIM), BF16),
        compiler_params=_params(32, 2),
        name="na_attn",
    )(qkv, qkv, qkv, bias)


WG_CHUNK_BLOCKS = 8
WG_KEYS = 3 * WG_BLOCK


def _wg_body(hs_ref, q_ref, k_ref, v_ref, o_ref):
    kv = pl.program_id(0)
    chunk = pl.program_id(1)

    def block(j, carry):
        n = chunk * WG_CHUNK_BLOCKS + j
        start = jnp.clip((n - 1) * WG_BLOCK, 0, SEQ - WG_KEYS)
        ks = pl.ds(pl.multiple_of(start, WG_BLOCK), WG_KEYS)
        qs = pl.ds(pl.multiple_of(j * WG_BLOCK, WG_BLOCK), WG_BLOCK)
        kw = k_ref[ks, :]
        vw = v_ref[ks, :]
        q_pos = n * WG_BLOCK + lax.broadcasted_iota(jnp.int32, (WG_BLOCK, WG_KEYS), 0)
        k_pos = start + lax.broadcasted_iota(jnp.int32, (WG_BLOCK, WG_KEYS), 1)
        dist = jnp.abs(k_pos - q_pos)
        valid = dist <= WG_WINDOW
        dist_f = dist.astype(F32)
        for g in range(WG_GROUP):
            h = kv * WG_GROUP + g
            sink = hs_ref[0, h]
            slope = hs_ref[1, h]
            cols = slice(g * HEAD_DIM, (g + 1) * HEAD_DIM)
            s = lax.dot_general(q_ref[qs, cols], kw, (((1,), (1,)), ((), ())), preferred_element_type=F32)
            s = jnp.where(valid, s * ATTN_SCALE - slope * dist_f, MASKED)
            m = jnp.maximum(jnp.max(s, axis=-1, keepdims=True), sink)
            e = jnp.exp(s - m)
            den = jnp.sum(e, axis=-1, keepdims=True) + jnp.exp(sink - m)
            o = jnp.dot(e.astype(BF16), vw, preferred_element_type=F32) / den
            o_ref[qs, cols] = o.astype(o_ref.dtype)
        return carry

    lax.fori_loop(0, WG_CHUNK_BLOCKS, block, 0)


def _windowed_gqa(qkv, head_scalars):
    tq = WG_CHUNK_BLOCKS * WG_BLOCK
    gw = WG_GROUP * HEAD_DIM
    return pl.pallas_call(
        _wg_body,
        grid=(WG_KV_HEADS, SEQ // tq),
        in_specs=[
            pl.BlockSpec(memory_space=pltpu.SMEM),
            pl.BlockSpec((tq, gw), lambda k, c: (c, WG_Q_BLK + k)),
            pl.BlockSpec((SEQ, HEAD_DIM), lambda k, c: (0, WG_K_BLK + k)),
            pl.BlockSpec((SEQ, HEAD_DIM), lambda k, c: (0, WG_V_BLK + k)),
        ],
        out_specs=pl.BlockSpec((tq, gw), lambda k, c: (c, k)),
        out_shape=jax.ShapeDtypeStruct((SEQ, WG_HEADS * HEAD_DIM), BF16),
        compiler_params=_params(32, 2),
        name="wg_attn",
    )(head_scalars, qkv, qkv, qkv)


CONV_TM = 512
HALO = 8


def _conv_body(bg_ref, cg_ref, hc_ref, cgp_ref, hcp_ref, cgn_ref, hcn_ref, w_ref, o_ref):
    i = pl.program_id(0)
    u = cg_ref[...] * hc_ref[...]
    prev_row = cgp_ref[HALO - 1:HALO, :] * hcp_ref[HALO - 1:HALO, :]
    next_row = cgn_ref[0:1, :] * hcn_ref[0:1, :]
    prev_row = jnp.where(i == 0, 0.0, prev_row)
    next_row = jnp.where(i == pl.num_programs(0) - 1, 0.0, next_row)
    row = lax.broadcasted_iota(jnp.int32, u.shape, 0)
    u_prev = jnp.where(row == 0, prev_row, pltpu.roll(u, 1, 0))
    u_next = jnp.where(row == CONV_TM - 1, next_row, pltpu.roll(u, CONV_TM - 1, 0))
    y = bg_ref[...] * (w_ref[0:1, :] * u_prev + w_ref[1:2, :] * u + w_ref[2:3, :] * u_next)
    o_ref[...] = y.astype(o_ref.dtype)


def _short_conv(rest, conv_w):
    nb = CONV_TM // HALO
    last = SEQ // HALO - 1
    main = lambda col: pl.BlockSpec((CONV_TM, SC_WIDTH), lambda i: (i, col))
    prev = lambda col: pl.BlockSpec((HALO, SC_WIDTH), lambda i: (jnp.maximum(i * nb - 1, 0), col))
    nxt = lambda col: pl.BlockSpec((HALO, SC_WIDTH), lambda i: (jnp.minimum((i + 1) * nb, last), col))
    return pl.pallas_call(
        _conv_body,
        grid=(SEQ // CONV_TM,),
        in_specs=[main(0), main(1), main(2), prev(1), prev(2), nxt(1), nxt(2),
                  pl.BlockSpec((3, SC_WIDTH), lambda i: (0, 0))],
        out_specs=pl.BlockSpec((CONV_TM, SC_WIDTH), lambda i: (i, 0)),
        out_shape=jax.ShapeDtypeStruct((SEQ, SC_WIDTH), BF16),
        compiler_params=_params(40, 1),
        name="short_conv",
    )(rest, rest, rest, rest, rest, rest, rest, conv_w)


MERGE_TM = 512
MERGE_TN = 512


def _merge_body(l_ref, ya_ref, yb_ref, yc_ref, g0_ref, g1_ref, g2_ref, b0_ref, b1_ref, b2_ref, w_ref,
                o_ref, wb_ref):
    @pl.when(pl.program_id(1) == 0)
    def _():
        for n in range(N_BRANCH):
            _cast_into(w_ref.at[n], wb_ref.at[n], BRANCH_WIDTH)

    acc = jnp.zeros((MERGE_TM, MERGE_TN), F32)
    for n, (y_ref, g_ref, b_ref) in enumerate(((ya_ref, g0_ref, b0_ref), (yb_ref, g1_ref, b1_ref),
                                               (yc_ref, g2_ref, b2_ref))):
        branch = jnp.dot(y_ref[...], wb_ref[n], preferred_element_type=F32)
        acc = acc + jax.nn.sigmoid(g_ref[...] + b_ref[...]) * branch
    o_ref[...] = acc.astype(o_ref.dtype)


def _merge(l, ya, yb, yc, rest, b_gate, w_branch):
    y_spec = pl.BlockSpec((MERGE_TM, BRANCH_WIDTH), lambda j, i, l_ref: (i, 0))
    gcol = lambda n: (GATE_COL + n * D_MODEL) // MERGE_TN
    g_spec = lambda n: pl.BlockSpec((MERGE_TM, MERGE_TN), lambda j, i, l_ref: (i, gcol(n) + j))
    b_spec = lambda n: pl.BlockSpec((1, MERGE_TN), lambda j, i, l_ref: (0, n * D_MODEL // MERGE_TN + j))
    return pl.pallas_call(
        _merge_body,
        grid_spec=pltpu.PrefetchScalarGridSpec(
            num_scalar_prefetch=1,
            grid=(D_MODEL // MERGE_TN, SEQ // MERGE_TM),
            in_specs=[y_spec, y_spec, y_spec, g_spec(0), g_spec(1), g_spec(2), b_spec(0), b_spec(1), b_spec(2),
                      pl.BlockSpec((None, N_BRANCH, BRANCH_WIDTH, MERGE_TN),
                                   lambda j, i, l_ref: (l_ref[0], 0, 0, j))],
            out_specs=pl.BlockSpec((MERGE_TM, MERGE_TN), lambda j, i, l_ref: (i, j)),
            scratch_shapes=[pltpu.VMEM((N_BRANCH, BRANCH_WIDTH, MERGE_TN), BF16)],
        ),
        out_shape=jax.ShapeDtypeStruct((SEQ, D_MODEL), BF16),
        compiler_params=_params(48, 2),
        name="branch_merge",
    )(l, ya, yb, yc, rest, rest, rest, b_gate, b_gate, b_gate, w_branch)


def _layer_norm(y, g, b):
    mu = jnp.mean(y, axis=-1, keepdims=True)
    yc = y - mu
    var = jnp.mean(yc * yc, axis=-1, keepdims=True)
    return yc * lax.rsqrt(var + LN_EPS) * g + b


OUT_TM = 256


def _out_body(m_ref, w_ref, x_ref, g_ref, b_ref, wr_ref, xo_ref, xb_ref, aff_ref):
    mix = jnp.dot(m_ref[...], w_ref[...], preferred_element_type=F32)
    x1 = _layer_norm(ALPHA * x_ref[...] + mix, g_ref[...], b_ref[...])
    xo_ref[...] = x1
    x1b = x1.astype(BF16)
    xb_ref[...] = x1b
    logits = lax.dot_general(wr_ref[...], x1b, (((1,), (1,)), ((), ())), preferred_element_type=F32)
    e = jnp.exp(logits - jnp.max(logits, axis=0, keepdims=True))
    aff_ref[...] = e / jnp.sum(e, axis=0, keepdims=True)


def _out_proj_ln(merged, w_out_b, x, g, b, w_router_t):
    row = lambda n, dt=None: pl.BlockSpec((OUT_TM, n), lambda i: (i, 0))
    full = lambda a: pl.BlockSpec(a.shape, lambda i: (0,) * a.ndim)
    return pl.pallas_call(
        _out_body,
        grid=(SEQ // OUT_TM,),
        in_specs=[row(D_MODEL), full(w_out_b), row(D_MODEL), full(g), full(b), full(w_router_t)],
        out_specs=[row(D_MODEL), row(D_MODEL), pl.BlockSpec((N_EXPERTS, OUT_TM), lambda i: (0, i))],
        out_shape=[jax.ShapeDtypeStruct((SEQ, D_MODEL), F32), jax.ShapeDtypeStruct((SEQ, D_MODEL), BF16),
                   jax.ShapeDtypeStruct((N_EXPERTS, SEQ), F32)],
        compiler_params=_params(48, 1),
        name="out_proj_ln",
    )(merged, w_out_b, x, g, b, w_router_t)


FINAL_TM = 512


def _final_body(x_ref, moe_ref, g_ref, b_ref, xo_ref, xb_ref):
    x2 = _layer_norm(ALPHA * x_ref[...] + moe_ref[...], g_ref[...], b_ref[...])
    xo_ref[...] = x2
    xb_ref[...] = x2.astype(BF16)


def _residual_ln(x, moe, g, b):
    row = pl.BlockSpec((FINAL_TM, D_MODEL), lambda i: (i, 0))
    vec = pl.BlockSpec((1, D_MODEL), lambda i: (0, 0))
    return pl.pallas_call(
        _final_body,
        grid=(SEQ // FINAL_TM,),
        in_specs=[row, row, vec, vec],
        out_specs=[row, row],
        out_shape=[jax.ShapeDtypeStruct((SEQ, D_MODEL), F32), jax.ShapeDtypeStruct((SEQ, D_MODEL), BF16)],
        compiler_params=_params(48, 1),
        name="residual_ln",
    )(x, moe, g, b)


FFN_TF = 256


def _ffn_body(l_ref, xe_ref, gate_ref, wg_ref, wu_ref, wd_ref, o_ref):
    f = pl.program_id(1)
    xe = xe_ref[...]
    a = jnp.dot(xe, wg_ref[...].astype(BF16), preferred_element_type=F32)
    u = jnp.dot(xe, wu_ref[...].astype(BF16), preferred_element_type=F32)
    hid = (jax.nn.silu(a) * u).astype(BF16)
    part = jnp.dot(hid, wd_ref[...].astype(BF16), preferred_element_type=F32)

    @pl.when(f == 0)
    def _():
        o_ref[...] = part

    @pl.when(f > 0)
    def _():
        o_ref[...] += part

    @pl.when(f == pl.num_programs(1) - 1)
    def _():
        o_ref[...] = o_ref[...] * gate_ref[...]


def _expert_ffn(l, xe, gates, w_gate, w_up, w_down):
    return pl.pallas_call(
        _ffn_body,
        grid_spec=pltpu.PrefetchScalarGridSpec(
            num_scalar_prefetch=1,
            grid=(N_EXPERTS, D_FF // FFN_TF),
            in_specs=[
                pl.BlockSpec((None, CAPACITY, D_MODEL), lambda e, f, l_ref: (e, 0, 0)),
                pl.BlockSpec((None, CAPACITY, 1), lambda e, f, l_ref: (e, 0, 0)),
                pl.BlockSpec((None, None, D_MODEL, FFN_TF), lambda e, f, l_ref: (l_ref[0], e, 0, f)),
                pl.BlockSpec((None, None, D_MODEL, FFN_TF), lambda e, f, l_ref: (l_ref[0], e, 0, f)),
                pl.BlockSpec((None, None, FFN_TF, D_MODEL), lambda e, f, l_ref: (l_ref[0], e, f, 0)),
            ],
            out_specs=pl.BlockSpec((None, CAPACITY, D_MODEL), lambda e, f, l_ref: (e, 0, 0)),
        ),
        out_shape=jax.ShapeDtypeStruct((N_EXPERTS, CAPACITY, D_MODEL), F32),
        compiler_params=_params(56, 2),
        name="expert_ffn",
    )(l, xe, gates, w_gate, w_up, w_down)


def kernel(x, w_in, b_gate, rpb, sink, conv_w, w_branch, w_out, ln_g, ln_b, w_router, w_gate, w_up, w_down):
    x0 = x.reshape(SEQ, D_MODEL)
    na_bias = _na_bias_tables(rpb)
    slopes = 2.0 ** (-8.0 * jnp.arange(1, WG_HEADS + 1, dtype=F32) / WG_HEADS)
    head_scalars = jnp.stack([sink, jnp.broadcast_to(slopes, sink.shape)], axis=1)
    w_router_t = jnp.swapaxes(w_router, 1, 2).astype(BF16)

    def layer(carry, per_layer):
        xf, xb = carry
        li, bias_l, hs_l, conv_l, bg_l, g_l, b_l, wr_l = per_layer
        l = li.reshape(1)
        qkv = _project(l, xb, w_in, 0, QKV_WIDTH, BF16)
        rest = _project(l, xb, w_in, QKV_WIDTH, REST_WIDTH, F32)
        ya = _neighbourhood_attention(qkv, bias_l)
        yb = _windowed_gqa(qkv, hs_l)
        yc = _short_conv(rest, conv_l)
        merged = _merge(l, ya, yb, yc, rest, bg_l.reshape(1, N_BRANCH * D_MODEL), w_branch)
        w_out_b = _cast_layer(l, w_out, 512)
        x1, x1b, aff_t = _out_proj_ln(merged, w_out_b, xf, g_l[0:1], b_l[0:1], wr_l)
        gates, idx = lax.top_k(aff_t, CAPACITY)
        xe = x1b[idx]
        ye = _expert_ffn(l, xe, gates[..., None], w_gate, w_up, w_down)
        moe = jnp.zeros((SEQ, D_MODEL), F32).at[idx.reshape(-1)].add(ye.reshape(-1, D_MODEL))
        x2, x2b = _residual_ln(x1, moe, g_l[1:2], b_l[1:2])
        return (x2, x2b), None

    per_layer = (jnp.arange(DEPTH, dtype=jnp.int32), na_bias, head_scalars, conv_w, b_gate, ln_g, ln_b, w_router_t)
    (xf, _), _ = lax.scan(layer, (x0, _cast_rows(x0, 512)), per_layer)
    return xf.reshape(x.shape)
```

```python
import functools

import jax
import jax.numpy as jnp
from jax import lax
from jax.experimental import pallas as pl
from jax.experimental.pallas import tpu as pltpu

F32 = jnp.float32
BF16 = jnp.bfloat16

D_MODEL = 2048
SEQ = 8192
DEPTH = 4
GRID_W = 64
ROWS = SEQ // GRID_W
HEAD_DIM = 128
NA_HEADS = 8
NA_ROWS = 8
NA_COLS = 16
WG_HEADS = 8
WG_KV_HEADS = 2
WG_GROUP = WG_HEADS // WG_KV_HEADS
WG_WINDOW = 128
WG_BLOCK = 128
SC_WIDTH = 1024
BRANCH_WIDTH = 1024
N_BRANCH = 3
N_EXPERTS = 16
CAPACITY = 2 * SEQ // N_EXPERTS
D_FF = 1536
ALPHA = (2 * DEPTH) ** 0.25
LN_EPS = 1e-5
ATTN_SCALE = HEAD_DIM ** -0.5
MASKED = -1e30

QKV_WIDTH = 3 * NA_HEADS * HEAD_DIM + (WG_HEADS + 2 * WG_KV_HEADS) * HEAD_DIM
REST_WIDTH = 3 * SC_WIDTH + N_BRANCH * D_MODEL
NA_Q_BLK, NA_K_BLK, NA_V_BLK = 0, NA_HEADS, 2 * NA_HEADS
WG_Q_BLK = 3 * NA_HEADS * HEAD_DIM // (WG_GROUP * HEAD_DIM)
WG_K_BLK = (3 * NA_HEADS + WG_HEADS) * HEAD_DIM // HEAD_DIM
WG_V_BLK = WG_K_BLK + WG_KV_HEADS
GATE_COL = 3 * SC_WIDTH

MIB = 1024 * 1024


def _params(vmem_mib, n_axes):
    return pltpu.CompilerParams(dimension_semantics=("arbitrary",) * n_axes,
                                vmem_limit_bytes=vmem_mib * MIB)


def _cast_body(x_ref, o_ref):
    o_ref[...] = x_ref[...].astype(BF16)


def _cast_rows(x, tm):
    m, n = x.shape
    return pl.pallas_call(
        _cast_body,
        grid=(m // tm,),
        in_specs=[pl.BlockSpec((tm, n), lambda i: (i, 0))],
        out_specs=pl.BlockSpec((tm, n), lambda i: (i, 0)),
        out_shape=jax.ShapeDtypeStruct((m, n), BF16),
        compiler_params=_params(32, 1),
        name="cast_rows",
    )(x)


def _cast_layer_body(l_ref, x_ref, o_ref):
    o_ref[...] = x_ref[...].astype(BF16)


def _cast_layer(l, w, tm):
    _, m, n = w.shape
    return pl.pallas_call(
        _cast_layer_body,
        grid_spec=pltpu.PrefetchScalarGridSpec(
            num_scalar_prefetch=1,
            grid=(m // tm,),
            in_specs=[pl.BlockSpec((None, tm, n), lambda i, l_ref: (l_ref[0], i, 0))],
            out_specs=pl.BlockSpec((tm, n), lambda i, l_ref: (i, 0)),
        ),
        out_shape=jax.ShapeDtypeStruct((m, n), BF16),
        compiler_params=_params(32, 1),
        name="cast_layer",
    )(l, w)


PROJ_TM = 1024
PROJ_TN = 768
CAST_ROWS = 256


def _cast_into(w_ref, wb_ref, rows):
    def chunk(i, c):
        sl = pl.ds(pl.multiple_of(i * CAST_ROWS, CAST_ROWS), CAST_ROWS)
        wb_ref[sl, :] = w_ref[sl, :].astype(BF16)
        return c
    lax.fori_loop(0, rows // CAST_ROWS, chunk, 0)


def _proj_body(l_ref, x_ref, w_ref, o_ref, wb_ref):
    @pl.when(pl.program_id(1) == 0)
    def _():
        _cast_into(w_ref, wb_ref, D_MODEL)

    o_ref[...] = jnp.dot(x_ref[...], wb_ref[...], preferred_element_type=F32).astype(o_ref.dtype)


def _project(l, xb, w_in, col0, width, out_dtype):
    n_tiles, off = width // PROJ_TN, col0 // PROJ_TN
    return pl.pallas_call(
        _proj_body,
        grid_spec=pltpu.PrefetchScalarGridSpec(
            num_scalar_prefetch=1,
            grid=(n_tiles, SEQ // PROJ_TM),
            in_specs=[
                pl.BlockSpec((PROJ_TM, D_MODEL), lambda j, i, l_ref: (i, 0)),
                pl.BlockSpec((None, D_MODEL, PROJ_TN), lambda j, i, l_ref: (l_ref[0], 0, off + j)),
            ],
            out_specs=pl.BlockSpec((PROJ_TM, PROJ_TN), lambda j, i, l_ref: (i, j)),
            scratch_shapes=[pltpu.VMEM((D_MODEL, PROJ_TN), BF16)],
        ),
        out_shape=jax.ShapeDtypeStruct((SEQ, width), out_dtype),
        compiler_params=_params(48, 2),
        name="in_proj",
    )(l, xb, w_in)


NA_CHUNK_ROWS = 16
NA_GROUP = 8
NA_KEYS = NA_ROWS * GRID_W


def _na_bias_tables(rpb):
    c = jnp.arange(GRID_W)[:, None]
    cj = jnp.arange(GRID_W)[None, :]
    cs = jnp.clip(c - NA_COLS // 2, 0, GRID_W - NA_COLS)
    valid = (cj >= cs) & (cj < cs + NA_COLS)
    select = ((cj - c + (NA_COLS - 1))[None] == jnp.arange(2 * NA_COLS - 1)[:, None, None]) & valid[None]
    t = jnp.einsum('lhrk,kcj->lhcrj', rpb.astype(F32), select.astype(F32),
                   precision=lax.Precision.HIGHEST)
    t = jnp.where(valid[None, None, :, None, :], t, MASKED)
    per_d = [t[:, :, :, NA_ROWS - 1 - d:2 * NA_ROWS - 1 - d, :].reshape(DEPTH, NA_HEADS, GRID_W, NA_KEYS)
             for d in range(NA_ROWS)]
    return jnp.stack(per_d, axis=1)


def _na_body(q_ref, k_ref, v_ref, bias_ref, o_ref):
    chunk = pl.program_id(1)

    def group(gi, carry):
        qss, kss, logits = [], [], []
        for t in range(NA_GROUP):
            i = gi * NA_GROUP + t
            r = chunk * NA_CHUNK_ROWS + i
            rs = jnp.clip(r - NA_ROWS // 2, 0, ROWS - NA_ROWS)
            qs = pl.ds(pl.multiple_of(i * GRID_W, GRID_W), GRID_W)
            ks = pl.ds(pl.multiple_of(rs * GRID_W, GRID_W), NA_KEYS)
            s = lax.dot_general(q_ref[qs, :], k_ref[ks, :], (((1,), (1,)), ((), ())),
                                preferred_element_type=F32)
            logits.append(s * ATTN_SCALE + bias_ref[r - rs])
            qss.append(qs)
            kss.append(ks)
        probs, dens = [], []
        for s in logits:
            e = jnp.exp(s - jnp.max(s, axis=-1, keepdims=True))
            dens.append(jnp.sum(e, axis=-1, keepdims=True))
            probs.append(e.astype(BF16))
        for qs, ks, p, den in zip(qss, kss, probs, dens):
            o = jnp.dot(p, v_ref[ks, :], preferred_element_type=F32) / den
            o_ref[qs, :] = o.astype(o_ref.dtype)
        return carry

    lax.fori_loop(0, NA_CHUNK_ROWS // NA_GROUP, group, 0)


def _neighbourhood_attention(qkv, bias):
    tq = NA_CHUNK_ROWS * GRID_W
    return pl.pallas_call(
        _na_body,
        grid=(NA_HEADS, ROWS // NA_CHUNK_ROWS),
        in_specs=[
            pl.BlockSpec((tq, HEAD_DIM), lambda h, c: (c, NA_Q_BLK + h)),
            pl.BlockSpec((SEQ, HEAD_DIM), lambda h, c: (0, NA_K_BLK + h)),
            pl.BlockSpec((SEQ, HEAD_DIM), lambda h, c: (0, NA_V_BLK + h)),
            pl.BlockSpec((NA_ROWS, None, GRID_W, NA_KEYS), lambda h, c: (0, h, 0, 0)),
        ],
        out_specs=pl.BlockSpec((tq, HEAD_DIM), lambda h, c: (c, h)),
        out_shape=jax.ShapeDtypeStruct((SEQ, NA_HEADS * HEAD_DIM), BF16),
        compiler_params=_params(32, 2),
        name="na_attn",
    )(qkv, qkv, qkv, bias)


WG_CHUNK_BLOCKS = 8
WG_KEYS = 3 * WG_BLOCK


def _wg_body(hs_ref, q_ref, k_ref, v_ref, o_ref):
    kv = pl.program_id(0)
    chunk = pl.program_id(1)

    def block(j, carry):
        n = chunk * WG_CHUNK_BLOCKS + j
        start = jnp.clip((n - 1) * WG_BLOCK, 0, SEQ - WG_KEYS)
        ks = pl.ds(pl.multiple_of(start, WG_BLOCK), WG_KEYS)
        qs = pl.ds(pl.multiple_of(j * WG_BLOCK, WG_BLOCK), WG_BLOCK)
        kw = k_ref[ks, :]
        vw = v_ref[ks, :]
        q_pos = n * WG_BLOCK + lax.broadcasted_iota(jnp.int32, (WG_BLOCK, WG_KEYS), 0)
        k_pos = start + lax.broadcasted_iota(jnp.int32, (WG_BLOCK, WG_KEYS), 1)
        dist = jnp.abs(k_pos - q_pos)
        valid = dist <= WG_WINDOW
        dist_f = dist.astype(F32)
        heads = [(slice(g * HEAD_DIM, (g + 1) * HEAD_DIM), hs_ref[0, kv * WG_GROUP + g], hs_ref[1, kv * WG_GROUP + g])
                 for g in range(WG_GROUP)]
        logits = []
        for cols, sink, slope in heads:
            s = lax.dot_general(q_ref[qs, cols], kw, (((1,), (1,)), ((), ())), preferred_element_type=F32)
            logits.append(jnp.where(valid, s * ATTN_SCALE - slope * dist_f, MASKED))
        probs, dens = [], []
        for s, (cols, sink, slope) in zip(logits, heads):
            m = jnp.maximum(jnp.max(s, axis=-1, keepdims=True), sink)
            e = jnp.exp(s - m)
            dens.append(jnp.sum(e, axis=-1, keepdims=True) + jnp.exp(sink - m))
            probs.append(e.astype(BF16))
        for p, den, (cols, sink, slope) in zip(probs, dens, heads):
            o = jnp.dot(p, vw, preferred_element_type=F32) / den
            o_ref[qs, cols] = o.astype(o_ref.dtype)
        return carry

    lax.fori_loop(0, WG_CHUNK_BLOCKS, block, 0)


def _windowed_gqa(qkv, head_scalars):
    tq = WG_CHUNK_BLOCKS * WG_BLOCK
    gw = WG_GROUP * HEAD_DIM
    return pl.pallas_call(
        _wg_body,
        grid=(WG_KV_HEADS, SEQ // tq),
        in_specs=[
            pl.BlockSpec(memory_space=pltpu.SMEM),
            pl.BlockSpec((tq, gw), lambda k, c: (c, WG_Q_BLK + k)),
            pl.BlockSpec((SEQ, HEAD_DIM), lambda k, c: (0, WG_K_BLK + k)),
            pl.BlockSpec((SEQ, HEAD_DIM), lambda k, c: (0, WG_V_BLK + k)),
        ],
        out_specs=pl.BlockSpec((tq, gw), lambda k, c: (c, k)),
        out_shape=jax.ShapeDtypeStruct((SEQ, WG_HEADS * HEAD_DIM), BF16),
        compiler_params=_params(32, 2),
        name="wg_attn",
    )(head_scalars, qkv, qkv, qkv)


CONV_TM = 512
HALO = 8


def _conv_body(bg_ref, cg_ref, hc_ref, cgp_ref, hcp_ref, cgn_ref, hcn_ref, w_ref, o_ref):
    i = pl.program_id(0)
    u = cg_ref[...] * hc_ref[...]
    prev_row = cgp_ref[HALO - 1:HALO, :] * hcp_ref[HALO - 1:HALO, :]
    next_row = cgn_ref[0:1, :] * hcn_ref[0:1, :]
    prev_row = jnp.where(i == 0, 0.0, prev_row)
    next_row = jnp.where(i == pl.num_programs(0) - 1, 0.0, next_row)
    row = lax.broadcasted_iota(jnp.int32, u.shape, 0)
    u_prev = jnp.where(row == 0, prev_row, pltpu.roll(u, 1, 0))
    u_next = jnp.where(row == CONV_TM - 1, next_row, pltpu.roll(u, CONV_TM - 1, 0))
    y = bg_ref[...] * (w_ref[0:1, :] * u_prev + w_ref[1:2, :] * u + w_ref[2:3, :] * u_next)
    o_ref[...] = y.astype(o_ref.dtype)


def _short_conv(rest, conv_w):
    nb = CONV_TM // HALO
    last = SEQ // HALO - 1
    main = lambda col: pl.BlockSpec((CONV_TM, SC_WIDTH), lambda i: (i, col))
    prev = lambda col: pl.BlockSpec((HALO, SC_WIDTH), lambda i: (jnp.maximum(i * nb - 1, 0), col))
    nxt = lambda col: pl.BlockSpec((HALO, SC_WIDTH), lambda i: (jnp.minimum((i + 1) * nb, last), col))
    return pl.pallas_call(
        _conv_body,
        grid=(SEQ // CONV_TM,),
        in_specs=[main(0), main(1), main(2), prev(1), prev(2), nxt(1), nxt(2),
                  pl.BlockSpec((3, SC_WIDTH), lambda i: (0, 0))],
        out_specs=pl.BlockSpec((CONV_TM, SC_WIDTH), lambda i: (i, 0)),
        out_shape=jax.ShapeDtypeStruct((SEQ, SC_WIDTH), BF16),
        compiler_params=_params(40, 1),
        name="short_conv",
    )(rest, rest, rest, rest, rest, rest, rest, conv_w)


MERGE_TM = 512
MERGE_TN = 512


def _merge_body(l_ref, ya_ref, yb_ref, yc_ref, g0_ref, g1_ref, g2_ref, b0_ref, b1_ref, b2_ref, w_ref,
                o_ref, wb_ref):
    @pl.when(pl.program_id(1) == 0)
    def _():
        for n in range(N_BRANCH):
            _cast_into(w_ref.at[n], wb_ref.at[n], BRANCH_WIDTH)

    acc = jnp.zeros((MERGE_TM, MERGE_TN), F32)
    for n, (y_ref, g_ref, b_ref) in enumerate(((ya_ref, g0_ref, b0_ref), (yb_ref, g1_ref, b1_ref),
                                               (yc_ref, g2_ref, b2_ref))):
        branch = jnp.dot(y_ref[...], wb_ref[n], preferred_element_type=F32)
        acc = acc + jax.nn.sigmoid(g_ref[...] + b_ref[...]) * branch
    o_ref[...] = acc.astype(o_ref.dtype)


def _merge(l, ya, yb, yc, rest, b_gate, w_branch):
    y_spec = pl.BlockSpec((MERGE_TM, BRANCH_WIDTH), lambda j, i, l_ref: (i, 0))
    gcol = lambda n: (GATE_COL + n * D_MODEL) // MERGE_TN
    g_spec = lambda n: pl.BlockSpec((MERGE_TM, MERGE_TN), lambda j, i, l_ref: (i, gcol(n) + j))
    b_spec = lambda n: pl.BlockSpec((1, MERGE_TN), lambda j, i, l_ref: (0, n * D_MODEL // MERGE_TN + j))
    return pl.pallas_call(
        _merge_body,
        grid_spec=pltpu.PrefetchScalarGridSpec(
            num_scalar_prefetch=1,
            grid=(D_MODEL // MERGE_TN, SEQ // MERGE_TM),
            in_specs=[y_spec, y_spec, y_spec, g_spec(0), g_spec(1), g_spec(2), b_spec(0), b_spec(1), b_spec(2),
                      pl.BlockSpec((None, N_BRANCH, BRANCH_WIDTH, MERGE_TN),
                                   lambda j, i, l_ref: (l_ref[0], 0, 0, j))],
            out_specs=pl.BlockSpec((MERGE_TM, MERGE_TN), lambda j, i, l_ref: (i, j)),
            scratch_shapes=[pltpu.VMEM((N_BRANCH, BRANCH_WIDTH, MERGE_TN), BF16)],
        ),
        out_shape=jax.ShapeDtypeStruct((SEQ, D_MODEL), BF16),
        compiler_params=_params(48, 2),
        name="branch_merge",
    )(l, ya, yb, yc, rest, rest, rest, b_gate, b_gate, b_gate, w_branch)


def _layer_norm(y, g, b):
    mu = jnp.mean(y, axis=-1, keepdims=True)
    yc = y - mu
    var = jnp.mean(yc * yc, axis=-1, keepdims=True)
    return yc * lax.rsqrt(var + LN_EPS) * g + b


OUT_TM = 256


def _out_body(m_ref, w_ref, x_ref, g_ref, b_ref, wr_ref, xo_ref, res_ref, aff_ref):
    mix = jnp.dot(m_ref[...], w_ref[...], preferred_element_type=F32)
    x1 = _layer_norm(ALPHA * x_ref[...] + mix, g_ref[...], b_ref[...])
    xo_ref[...] = x1
    res_ref[...] = ALPHA * x1
    logits = lax.dot_general(wr_ref[...], x1.astype(BF16), (((1,), (1,)), ((), ())),
                             preferred_element_type=F32)
    e = jnp.exp(logits - jnp.max(logits, axis=0, keepdims=True))
    aff_ref[...] = e / jnp.sum(e, axis=0, keepdims=True)


def _out_proj_ln(merged, w_out_b, x, g, b, w_router_t):
    row = pl.BlockSpec((OUT_TM, D_MODEL), lambda i: (i, 0))
    full = lambda a: pl.BlockSpec(a.shape, lambda i: (0,) * a.ndim)
    return pl.pallas_call(
        _out_body,
        grid=(SEQ // OUT_TM,),
        in_specs=[row, full(w_out_b), row, full(g), full(b), full(w_router_t)],
        out_specs=[row, row, pl.BlockSpec((N_EXPERTS, OUT_TM), lambda i: (0, i))],
        out_shape=[jax.ShapeDtypeStruct((SEQ, D_MODEL), F32), jax.ShapeDtypeStruct((SEQ, D_MODEL), F32),
                   jax.ShapeDtypeStruct((N_EXPERTS, SEQ), F32)],
        compiler_params=_params(48, 1),
        name="out_proj_ln",
    )(merged, w_out_b, x, g, b, w_router_t)


FINAL_TM = 512


def _final_body(y_ref, g_ref, b_ref, xo_ref, xb_ref):
    x2 = _layer_norm(y_ref[...], g_ref[...], b_ref[...])
    xo_ref[...] = x2
    xb_ref[...] = x2.astype(BF16)


def _final_ln(y, g, b):
    row = pl.BlockSpec((FINAL_TM, D_MODEL), lambda i: (i, 0))
    vec = pl.BlockSpec((1, D_MODEL), lambda i: (0, 0))
    return pl.pallas_call(
        _final_body,
        grid=(SEQ // FINAL_TM,),
        in_specs=[row, vec, vec],
        out_specs=[row, row],
        out_shape=[jax.ShapeDtypeStruct((SEQ, D_MODEL), F32), jax.ShapeDtypeStruct((SEQ, D_MODEL), BF16)],
        compiler_params=_params(48, 1),
        name="final_ln",
    )(y, g, b)


FFN_TF = 256
FFN_STEPS = D_FF // FFN_TF
GATHER_STEPS = 4
GATHER_ROWS = CAPACITY // GATHER_STEPS
DOWN_HALF = D_MODEL // 2


def _row_copy(x_hbm, rows_ref, sem, src_row, dst_row):
    return pltpu.make_async_copy(x_hbm.at[pl.ds(src_row, 1), :], rows_ref.at[pl.ds(dst_row, 1), :], sem)


def _start_rows(x_hbm, rows_ref, sem, idx_ref, first, count):
    def issue(j, c):
        _row_copy(x_hbm, rows_ref, sem, idx_ref[0, first + j], first + j).start()
        return c
    lax.fori_loop(0, count, issue, 0, unroll=8)


def _wait_rows(x_hbm, rows_ref, sem, count):
    def wait(j, c):
        _row_copy(x_hbm, rows_ref, sem, 0, 0).wait()
        return c
    lax.fori_loop(0, count, wait, 0, unroll=8)


def _ffn_body(l_ref, idx_ref, idx_next_ref, x_hbm, res_hbm, gate_ref, wg_ref, wu_ref, wd_ref, acc_hbm,
              rows_ref, xe_ref, ffn_ref, accb_ref, sem_x, sem_g, sem_s):
    del res_hbm
    e = pl.program_id(0)
    f = pl.program_id(1)
    last = f == FFN_STEPS - 1

    @pl.when((e == 0) & (f == 0))
    def _():
        _start_rows(x_hbm, rows_ref, sem_x, idx_ref, 0, CAPACITY)

    @pl.when(f == 0)
    def _():
        _wait_rows(x_hbm, rows_ref, sem_x, CAPACITY)
        _cast_into(rows_ref, xe_ref, CAPACITY)

    @pl.when((e + 1 < N_EXPERTS) & (f < GATHER_STEPS))
    def _():
        _start_rows(x_hbm, rows_ref, sem_x, idx_next_ref, f * GATHER_ROWS, GATHER_ROWS)

    @pl.when((e > 0) & (f == 1))
    def _():
        _wait_rows(accb_ref, acc_hbm, sem_s, CAPACITY)

    @pl.when((f >= 1) & (f <= GATHER_STEPS))
    def _():
        _start_rows(acc_hbm, accb_ref, sem_g, idx_ref, (f - 1) * GATHER_ROWS, GATHER_ROWS)

    xe = xe_ref[...]
    a = jnp.dot(xe, wg_ref[...].astype(BF16), preferred_element_type=F32)
    u = jnp.dot(xe, wu_ref[...].astype(BF16), preferred_element_type=F32)
    hid = (jax.nn.silu(a) * u).astype(BF16)

    @pl.when(last)
    def _():
        _wait_rows(acc_hbm, accb_ref, sem_g, CAPACITY)

    for half in range(2):
        cols = slice(half * DOWN_HALF, (half + 1) * DOWN_HALF)
        part = jnp.dot(hid, wd_ref[:, cols].astype(BF16), preferred_element_type=F32)

        @pl.when(f == 0)
        def _():
            ffn_ref[:, cols] = part

        @pl.when((f > 0) & jnp.logical_not(last))
        def _():
            ffn_ref[:, cols] += part

        @pl.when(last)
        def _():
            accb_ref[:, cols] += (ffn_ref[:, cols] + part) * gate_ref[...]

    @pl.when(last)
    def _():
        def issue(j, c):
            _row_copy(accb_ref, acc_hbm, sem_s, j, idx_ref[0, j]).start()
            return c
        lax.fori_loop(0, CAPACITY, issue, 0, unroll=8)

    @pl.when(last & (e == N_EXPERTS - 1))
    def _():
        _wait_rows(accb_ref, acc_hbm, sem_s, CAPACITY)


def _expert_ffn_accumulate(l, idx, x1, res, gates, w_gate, w_up, w_down):
    cur = lambda e, f, l_ref: (e, 0, 0)
    nxt = lambda e, f, l_ref: (jnp.minimum(e + 1, N_EXPERTS - 1), 0, 0)
    rows = lambda dt: pltpu.VMEM((CAPACITY, D_MODEL), dt)
    return pl.pallas_call(
        _ffn_body,
        grid_spec=pltpu.PrefetchScalarGridSpec(
            num_scalar_prefetch=1,
            grid=(N_EXPERTS, FFN_STEPS),
            in_specs=[
                pl.BlockSpec((None, 1, CAPACITY), cur, memory_space=pltpu.SMEM),
                pl.BlockSpec((None, 1, CAPACITY), nxt, memory_space=pltpu.SMEM),
                pl.BlockSpec(memory_space=pl.ANY),
                pl.BlockSpec(memory_space=pl.ANY),
                pl.BlockSpec((None, CAPACITY, 1), cur),
                pl.BlockSpec((None, None, D_MODEL, FFN_TF), lambda e, f, l_ref: (l_ref[0], e, 0, f)),
                pl.BlockSpec((None, None, D_MODEL, FFN_TF), lambda e, f, l_ref: (l_ref[0], e, 0, f)),
                pl.BlockSpec((None, None, FFN_TF, D_MODEL), lambda e, f, l_ref: (l_ref[0], e, f, 0)),
            ],
            out_specs=pl.BlockSpec(memory_space=pl.ANY),
            scratch_shapes=[rows(F32), rows(BF16), rows(F32), rows(F32),
                            pltpu.SemaphoreType.DMA, pltpu.SemaphoreType.DMA, pltpu.SemaphoreType.DMA],
        ),
        out_shape=jax.ShapeDtypeStruct((SEQ, D_MODEL), F32),
        input_output_aliases={4: 0},
        compiler_params=_params(58, 2),
        name="expert_ffn",
    )(l, idx, idx, x1, res, gates, w_gate, w_up, w_down)


def kernel(x, w_in, b_gate, rpb, sink, conv_w, w_branch, w_out, ln_g, ln_b, w_router, w_gate, w_up, w_down):
    x0 = x.reshape(SEQ, D_MODEL)
    na_bias = _na_bias_tables(rpb)
    slopes = 2.0 ** (-8.0 * jnp.arange(1, WG_HEADS + 1, dtype=F32) / WG_HEADS)
    head_scalars = jnp.stack([sink, jnp.broadcast_to(slopes, sink.shape)], axis=1)
    w_router_t = jnp.swapaxes(w_router, 1, 2).astype(BF16)

    def layer(carry, per_layer):
        xf, xb = carry
        li, bias_l, hs_l, conv_l, bg_l, g_l, b_l, wr_l = per_layer
        l = li.reshape(1)
        qkv = _project(l, xb, w_in, 0, QKV_WIDTH, BF16)
        rest = _project(l, xb, w_in, QKV_WIDTH, REST_WIDTH, F32)
        ya = _neighbourhood_attention(qkv, bias_l)
        yb = _windowed_gqa(qkv, hs_l)
        yc = _short_conv(rest, conv_l)
        merged = _merge(l, ya, yb, yc, rest, bg_l.reshape(1, N_BRANCH * D_MODEL), w_branch)
        w_out_b = _cast_layer(l, w_out, 512)
        x1, res, aff_t = _out_proj_ln(merged, w_out_b, xf, g_l[0:1], b_l[0:1], wr_l)
        gates, idx = lax.top_k(aff_t, CAPACITY)
        y = _expert_ffn_accumulate(l, idx[:, None, :], x1, res, gates[..., None], w_gate, w_up, w_down)
        x2, x2b = _final_ln(y, g_l[1:2], b_l[1:2])
        return (x2, x2b), None

    per_layer = (jnp.arange(DEPTH, dtype=jnp.int32), na_bias, head_scalars, conv_w, b_gate, ln_g, ln_b, w_router_t)
    (xf, _), _ = lax.scan(layer, (x0, _cast_rows(x0, 512)), per_layer)
    return xf.reshape(x.shape)
```

```python
import functools

import jax
import jax.numpy as jnp
from jax import lax
from jax.experimental import pallas as pl
from jax.experimental.pallas import tpu as pltpu

F32 = jnp.float32
BF16 = jnp.bfloat16

D_MODEL = 2048
SEQ = 8192
DEPTH = 4
GRID_W = 64
ROWS = SEQ // GRID_W
HEAD_DIM = 128
NA_HEADS = 8
NA_ROWS = 8
NA_COLS = 16
WG_HEADS = 8
WG_KV_HEADS = 2
WG_GROUP = WG_HEADS // WG_KV_HEADS
WG_WINDOW = 128
WG_BLOCK = 128
SC_WIDTH = 1024
BRANCH_WIDTH = 1024
N_BRANCH = 3
N_EXPERTS = 16
CAPACITY = 2 * SEQ // N_EXPERTS
D_FF = 1536
ALPHA = (2 * DEPTH) ** 0.25
LN_EPS = 1e-5
ATTN_SCALE = HEAD_DIM ** -0.5
MASKED = -1e30

QKV_WIDTH = 3 * NA_HEADS * HEAD_DIM + (WG_HEADS + 2 * WG_KV_HEADS) * HEAD_DIM
REST_WIDTH = 3 * SC_WIDTH + N_BRANCH * D_MODEL
NA_Q_BLK, NA_K_BLK, NA_V_BLK = 0, NA_HEADS, 2 * NA_HEADS
WG_Q_BLK = 3 * NA_HEADS * HEAD_DIM // (WG_GROUP * HEAD_DIM)
WG_K_BLK = (3 * NA_HEADS + WG_HEADS) * HEAD_DIM // HEAD_DIM
WG_V_BLK = WG_K_BLK + WG_KV_HEADS
GATE_COL = 3 * SC_WIDTH

MIB = 1024 * 1024


def _params(vmem_mib, n_axes):
    return pltpu.CompilerParams(dimension_semantics=("arbitrary",) * n_axes,
                                vmem_limit_bytes=vmem_mib * MIB)


def _cast_body(x_ref, o_ref):
    o_ref[...] = x_ref[...].astype(BF16)


def _cast_rows(x, tm):
    m, n = x.shape
    return pl.pallas_call(
        _cast_body,
        grid=(m // tm,),
        in_specs=[pl.BlockSpec((tm, n), lambda i: (i, 0))],
        out_specs=pl.BlockSpec((tm, n), lambda i: (i, 0)),
        out_shape=jax.ShapeDtypeStruct((m, n), BF16),
        compiler_params=_params(32, 1),
        name="cast_rows",
    )(x)


def _cast_layer_body(l_ref, x_ref, o_ref):
    o_ref[...] = x_ref[...].astype(BF16)


def _cast_layer(l, w, tm):
    _, m, n = w.shape
    return pl.pallas_call(
        _cast_layer_body,
        grid_spec=pltpu.PrefetchScalarGridSpec(
            num_scalar_prefetch=1,
            grid=(m // tm,),
            in_specs=[pl.BlockSpec((None, tm, n), lambda i, l_ref: (l_ref[0], i, 0))],
            out_specs=pl.BlockSpec((tm, n), lambda i, l_ref: (i, 0)),
        ),
        out_shape=jax.ShapeDtypeStruct((m, n), BF16),
        compiler_params=_params(32, 1),
        name="cast_layer",
    )(l, w)


PROJ_TM = 1024
PROJ_TN = 768
CAST_ROWS = 256


def _cast_into(w_ref, wb_ref, rows):
    def chunk(i, c):
        sl = pl.ds(pl.multiple_of(i * CAST_ROWS, CAST_ROWS), CAST_ROWS)
        wb_ref[sl, :] = w_ref[sl, :].astype(BF16)
        return c
    lax.fori_loop(0, rows // CAST_ROWS, chunk, 0)


def _proj_body(l_ref, x_ref, w_ref, o_ref, wb_ref):
    @pl.when(pl.program_id(1) == 0)
    def _():
        _cast_into(w_ref, wb_ref, D_MODEL)

    o_ref[...] = jnp.dot(x_ref[...], wb_ref[...], preferred_element_type=F32).astype(o_ref.dtype)


def _project(l, xb, w_in, col0, width, out_dtype):
    n_tiles, off = width // PROJ_TN, col0 // PROJ_TN
    return pl.pallas_call(
        _proj_body,
        grid_spec=pltpu.PrefetchScalarGridSpec(
            num_scalar_prefetch=1,
            grid=(n_tiles, SEQ // PROJ_TM),
            in_specs=[
                pl.BlockSpec((PROJ_TM, D_MODEL), lambda j, i, l_ref: (i, 0)),
                pl.BlockSpec((None, D_MODEL, PROJ_TN), lambda j, i, l_ref: (l_ref[0], 0, off + j)),
            ],
            out_specs=pl.BlockSpec((PROJ_TM, PROJ_TN), lambda j, i, l_ref: (i, j)),
            scratch_shapes=[pltpu.VMEM((D_MODEL, PROJ_TN), BF16)],
        ),
        out_shape=jax.ShapeDtypeStruct((SEQ, width), out_dtype),
        compiler_params=_params(48, 2),
        name="in_proj",
    )(l, xb, w_in)


NA_CHUNK_ROWS = 16
NA_GROUP = 8
NA_KEYS = NA_ROWS * GRID_W


def _na_bias_tables(rpb):
    c = jnp.arange(GRID_W)[:, None]
    cj = jnp.arange(GRID_W)[None, :]
    cs = jnp.clip(c - NA_COLS // 2, 0, GRID_W - NA_COLS)
    valid = (cj >= cs) & (cj < cs + NA_COLS)
    select = ((cj - c + (NA_COLS - 1))[None] == jnp.arange(2 * NA_COLS - 1)[:, None, None]) & valid[None]
    t = jnp.einsum('lhrk,kcj->lhcrj', rpb.astype(F32), select.astype(F32),
                   precision=lax.Precision.HIGHEST)
    t = jnp.where(valid[None, None, :, None, :], t, MASKED)
    per_d = [t[:, :, :, NA_ROWS - 1 - d:2 * NA_ROWS - 1 - d, :].reshape(DEPTH, NA_HEADS, GRID_W, NA_KEYS)
             for d in range(NA_ROWS)]
    return jnp.stack(per_d, axis=1)


def _na_body(q_ref, k_ref, v_ref, bias_ref, o_ref):
    chunk = pl.program_id(1)

    def group(gi, carry):
        qss, kss, logits = [], [], []
        for t in range(NA_GROUP):
            i = gi * NA_GROUP + t
            r = chunk * NA_CHUNK_ROWS + i
            rs = jnp.clip(r - NA_ROWS // 2, 0, ROWS - NA_ROWS)
            qs = pl.ds(pl.multiple_of(i * GRID_W, GRID_W), GRID_W)
            ks = pl.ds(pl.multiple_of(rs * GRID_W, GRID_W), NA_KEYS)
            s = lax.dot_general(q_ref[qs, :], k_ref[ks, :], (((1,), (1,)), ((), ())),
                                preferred_element_type=F32)
            logits.append(s * ATTN_SCALE + bias_ref[r - rs])
            qss.append(qs)
            kss.append(ks)
        probs, dens = [], []
        for s in logits:
            e = jnp.exp(s - jnp.max(s, axis=-1, keepdims=True))
            dens.append(jnp.sum(e, axis=-1, keepdims=True))
            probs.append(e.astype(BF16))
        for qs, ks, p, den in zip(qss, kss, probs, dens):
            o = jnp.dot(p, v_ref[ks, :], preferred_element_type=F32) / den
            o_ref[qs, :] = o.astype(o_ref.dtype)
        return carry

    lax.fori_loop(0, NA_CHUNK_ROWS // NA_GROUP, group, 0)


def _neighbourhood_attention(qkv, bias):
    tq = NA_CHUNK_ROWS * GRID_W
    return pl.pallas_call(
        _na_body,
        grid=(NA_HEADS, ROWS // NA_CHUNK_ROWS),
        in_specs=[
            pl.BlockSpec((tq, HEAD_DIM), lambda h, c: (c, NA_Q_BLK + h)),
            pl.BlockSpec((SEQ, HEAD_DIM), lambda h, c: (0, NA_K_BLK + h)),
            pl.BlockSpec((SEQ, HEAD_DIM), lambda h, c: (0, NA_V_BLK + h)),
            pl.BlockSpec((NA_ROWS, None, GRID_W, NA_KEYS), lambda h, c: (0, h, 0, 0)),
        ],
        out_specs=pl.BlockSpec((tq, HEAD_DIM), lambda h, c: (c, h)),
        out_shape=jax.ShapeDtypeStruct((SEQ, NA_HEADS * HEAD_DIM), BF16),
        compiler_params=_params(32, 2),
        name="na_attn",
    )(qkv, qkv, qkv, bias)


WG_CHUNK_BLOCKS = 8
WG_KEYS = 3 * WG_BLOCK


def _wg_body(hs_ref, q_ref, k_ref, v_ref, o_ref):
    kv = pl.program_id(0)
    chunk = pl.program_id(1)

    def block(j, carry):
        n = chunk * WG_CHUNK_BLOCKS + j
        start = jnp.clip((n - 1) * WG_BLOCK, 0, SEQ - WG_KEYS)
        ks = pl.ds(pl.multiple_of(start, WG_BLOCK), WG_KEYS)
        qs = pl.ds(pl.multiple_of(j * WG_BLOCK, WG_BLOCK), WG_BLOCK)
        kw = k_ref[ks, :]
        vw = v_ref[ks, :]
        q_pos = n * WG_BLOCK + lax.broadcasted_iota(jnp.int32, (WG_BLOCK, WG_KEYS), 0)
        k_pos = start + lax.broadcasted_iota(jnp.int32, (WG_BLOCK, WG_KEYS), 1)
        dist = jnp.abs(k_pos - q_pos)
        valid = dist <= WG_WINDOW
        dist_f = dist.astype(F32)
        heads = [(slice(g * HEAD_DIM, (g + 1) * HEAD_DIM), hs_ref[0, kv * WG_GROUP + g], hs_ref[1, kv * WG_GROUP + g])
                 for g in range(WG_GROUP)]
        logits = []
        for cols, sink, slope in heads:
            s = lax.dot_general(q_ref[qs, cols], kw, (((1,), (1,)), ((), ())), preferred_element_type=F32)
            logits.append(jnp.where(valid, s * ATTN_SCALE - slope * dist_f, MASKED))
        probs, dens = [], []
        for s, (cols, sink, slope) in zip(logits, heads):
            m = jnp.maximum(jnp.max(s, axis=-1, keepdims=True), sink)
            e = jnp.exp(s - m)
            dens.append(jnp.sum(e, axis=-1, keepdims=True) + jnp.exp(sink - m))
            probs.append(e.astype(BF16))
        for p, den, (cols, sink, slope) in zip(probs, dens, heads):
            o = jnp.dot(p, vw, preferred_element_type=F32) / den
            o_ref[qs, cols] = o.astype(o_ref.dtype)
        return carry

    lax.fori_loop(0, WG_CHUNK_BLOCKS, block, 0)


def _windowed_gqa(qkv, head_scalars):
    tq = WG_CHUNK_BLOCKS * WG_BLOCK
    gw = WG_GROUP * HEAD_DIM
    return pl.pallas_call(
        _wg_body,
        grid=(WG_KV_HEADS, SEQ // tq),
        in_specs=[
            pl.BlockSpec(memory_space=pltpu.SMEM),
            pl.BlockSpec((tq, gw), lambda k, c: (c, WG_Q_BLK + k)),
            pl.BlockSpec((SEQ, HEAD_DIM), lambda k, c: (0, WG_K_BLK + k)),
            pl.BlockSpec((SEQ, HEAD_DIM), lambda k, c: (0, WG_V_BLK + k)),
        ],
        out_specs=pl.BlockSpec((tq, gw), lambda k, c: (c, k)),
        out_shape=jax.ShapeDtypeStruct((SEQ, WG_HEADS * HEAD_DIM), BF16),
        compiler_params=_params(32, 2),
        name="wg_attn",
    )(head_scalars, qkv, qkv, qkv)


CONV_TM = 512
HALO = 8


def _conv_body(bg_ref, cg_ref, hc_ref, cgp_ref, hcp_ref, cgn_ref, hcn_ref, w_ref, o_ref):
    i = pl.program_id(0)
    u = cg_ref[...] * hc_ref[...]
    prev_row = cgp_ref[HALO - 1:HALO, :] * hcp_ref[HALO - 1:HALO, :]
    next_row = cgn_ref[0:1, :] * hcn_ref[0:1, :]
    prev_row = jnp.where(i == 0, 0.0, prev_row)
    next_row = jnp.where(i == pl.num_programs(0) - 1, 0.0, next_row)
    row = lax.broadcasted_iota(jnp.int32, u.shape, 0)
    u_prev = jnp.where(row == 0, prev_row, pltpu.roll(u, 1, 0))
    u_next = jnp.where(row == CONV_TM - 1, next_row, pltpu.roll(u, CONV_TM - 1, 0))
    y = bg_ref[...] * (w_ref[0:1, :] * u_prev + w_ref[1:2, :] * u + w_ref[2:3, :] * u_next)
    o_ref[...] = y.astype(o_ref.dtype)


def _short_conv(rest, conv_w):
    nb = CONV_TM // HALO
    last = SEQ // HALO - 1
    main = lambda col: pl.BlockSpec((CONV_TM, SC_WIDTH), lambda i: (i, col))
    prev = lambda col: pl.BlockSpec((HALO, SC_WIDTH), lambda i: (jnp.maximum(i * nb - 1, 0), col))
    nxt = lambda col: pl.BlockSpec((HALO, SC_WIDTH), lambda i: (jnp.minimum((i + 1) * nb, last), col))
    return pl.pallas_call(
        _conv_body,
        grid=(SEQ // CONV_TM,),
        in_specs=[main(0), main(1), main(2), prev(1), prev(2), nxt(1), nxt(2),
                  pl.BlockSpec((3, SC_WIDTH), lambda i: (0, 0))],
        out_specs=pl.BlockSpec((CONV_TM, SC_WIDTH), lambda i: (i, 0)),
        out_shape=jax.ShapeDtypeStruct((SEQ, SC_WIDTH), BF16),
        compiler_params=_params(40, 1),
        name="short_conv",
    )(rest, rest, rest, rest, rest, rest, rest, conv_w)


MERGE_TM = 512
MERGE_TN = 512


def _merge_body(l_ref, ya_ref, yb_ref, yc_ref, g0_ref, g1_ref, g2_ref, b0_ref, b1_ref, b2_ref, w_ref,
                o_ref, wb_ref):
    @pl.when(pl.program_id(1) == 0)
    def _():
        for n in range(N_BRANCH):
            _cast_into(w_ref.at[n], wb_ref.at[n], BRANCH_WIDTH)

    acc = jnp.zeros((MERGE_TM, MERGE_TN), F32)
    for n, (y_ref, g_ref, b_ref) in enumerate(((ya_ref, g0_ref, b0_ref), (yb_ref, g1_ref, b1_ref),
                                               (yc_ref, g2_ref, b2_ref))):
        branch = jnp.dot(y_ref[...], wb_ref[n], preferred_element_type=F32)
        acc = acc + jax.nn.sigmoid(g_ref[...] + b_ref[...]) * branch
    o_ref[...] = acc.astype(o_ref.dtype)


def _merge(l, ya, yb, yc, rest, b_gate, w_branch):
    y_spec = pl.BlockSpec((MERGE_TM, BRANCH_WIDTH), lambda j, i, l_ref: (i, 0))
    gcol = lambda n: (GATE_COL + n * D_MODEL) // MERGE_TN
    g_spec = lambda n: pl.BlockSpec((MERGE_TM, MERGE_TN), lambda j, i, l_ref: (i, gcol(n) + j))
    b_spec = lambda n: pl.BlockSpec((1, MERGE_TN), lambda j, i, l_ref: (0, n * D_MODEL // MERGE_TN + j))
    return pl.pallas_call(
        _merge_body,
        grid_spec=pltpu.PrefetchScalarGridSpec(
            num_scalar_prefetch=1,
            grid=(D_MODEL // MERGE_TN, SEQ // MERGE_TM),
            in_specs=[y_spec, y_spec, y_spec, g_spec(0), g_spec(1), g_spec(2), b_spec(0), b_spec(1), b_spec(2),
                      pl.BlockSpec((None, N_BRANCH, BRANCH_WIDTH, MERGE_TN),
                                   lambda j, i, l_ref: (l_ref[0], 0, 0, j))],
            out_specs=pl.BlockSpec((MERGE_TM, MERGE_TN), lambda j, i, l_ref: (i, j)),
            scratch_shapes=[pltpu.VMEM((N_BRANCH, BRANCH_WIDTH, MERGE_TN), BF16)],
        ),
        out_shape=jax.ShapeDtypeStruct((SEQ, D_MODEL), BF16),
        compiler_params=_params(48, 2),
        name="branch_merge",
    )(l, ya, yb, yc, rest, rest, rest, b_gate, b_gate, b_gate, w_branch)


def _layer_norm(y, g, b):
    mu = jnp.mean(y, axis=-1, keepdims=True)
    yc = y - mu
    var = jnp.mean(yc * yc, axis=-1, keepdims=True)
    return yc * lax.rsqrt(var + LN_EPS) * g + b


OUT_TM = 256


def _out_body(m_ref, w_ref, x_ref, g_ref, b_ref, wr_ref, xo_ref, res_ref, aff_ref):
    mix = jnp.dot(m_ref[...], w_ref[...], preferred_element_type=F32)
    x1 = _layer_norm(ALPHA * x_ref[...] + mix, g_ref[...], b_ref[...])
    xo_ref[...] = x1
    res_ref[...] = ALPHA * x1
    logits = lax.dot_general(wr_ref[...], x1.astype(BF16), (((1,), (1,)), ((), ())),
                             preferred_element_type=F32)
    e = jnp.exp(logits - jnp.max(logits, axis=0, keepdims=True))
    aff_ref[...] = e / jnp.sum(e, axis=0, keepdims=True)


def _out_proj_ln(merged, w_out_b, x, g, b, w_router_t):
    row = pl.BlockSpec((OUT_TM, D_MODEL), lambda i: (i, 0))
    full = lambda a: pl.BlockSpec(a.shape, lambda i: (0,) * a.ndim)
    return pl.pallas_call(
        _out_body,
        grid=(SEQ // OUT_TM,),
        in_specs=[row, full(w_out_b), row, full(g), full(b), full(w_router_t)],
        out_specs=[row, row, pl.BlockSpec((N_EXPERTS, OUT_TM), lambda i: (0, i))],
        out_shape=[jax.ShapeDtypeStruct((SEQ, D_MODEL), F32), jax.ShapeDtypeStruct((SEQ, D_MODEL), F32),
                   jax.ShapeDtypeStruct((N_EXPERTS, SEQ), F32)],
        compiler_params=_params(48, 1),
        name="out_proj_ln",
    )(merged, w_out_b, x, g, b, w_router_t)


MIX_TM = 256
GATE_TN = 1024


def _mix_body(ya_ref, yb_ref, yc_ref, g00, g01, g10, g11, g20, g21, bg_ref, wb_ref, wo_ref, x_ref, g_ref, b_ref,
              wr_ref, xo_ref, res_ref, aff_ref):
    y_refs = (ya_ref, yb_ref, yc_ref)
    gate_refs = ((g00, g01), (g10, g11), (g20, g21))
    halves = []
    for half in range(D_MODEL // GATE_TN):
        acc = jnp.zeros((MIX_TM, GATE_TN), F32)
        for n in range(N_BRANCH):
            col0 = n * D_MODEL + half * GATE_TN
            branch = jnp.dot(y_refs[n][...],
                             wb_ref[n * BRANCH_WIDTH:(n + 1) * BRANCH_WIDTH, half * GATE_TN:(half + 1) * GATE_TN],
                             preferred_element_type=F32)
            gate = jax.nn.sigmoid(gate_refs[n][half][...] + bg_ref[:, col0:col0 + GATE_TN])
            acc = acc + gate * branch
        halves.append(acc.astype(BF16))
    mix = sum(jnp.dot(m, wo_ref[h * GATE_TN:(h + 1) * GATE_TN, :], preferred_element_type=F32)
              for h, m in enumerate(halves))
    x1 = _layer_norm(ALPHA * x_ref[...] + mix, g_ref[...], b_ref[...])
    xo_ref[...] = x1
    res_ref[...] = ALPHA * x1
    logits = lax.dot_general(wr_ref[...], x1.astype(BF16), (((1,), (1,)), ((), ())),
                             preferred_element_type=F32)
    e = jnp.exp(logits - jnp.max(logits, axis=0, keepdims=True))
    aff_ref[...] = e / jnp.sum(e, axis=0, keepdims=True)


def _mix_ln(ya, yb, yc, rest, b_gate, w_branch_b, w_out_b, x, g, b, w_router_t):
    row = lambda n: pl.BlockSpec((MIX_TM, n), lambda i: (i, 0))
    gate = lambda n, half: pl.BlockSpec(
        (MIX_TM, GATE_TN), lambda i: (i, (GATE_COL + n * D_MODEL) // GATE_TN + half))
    full = lambda a: pl.BlockSpec(a.shape, lambda i: (0,) * a.ndim)
    once = lambda a: pl.BlockSpec(a.shape, lambda i: (0,) * a.ndim, pipeline_mode=pl.Buffered(1))
    return pl.pallas_call(
        _mix_body,
        grid=(SEQ // MIX_TM,),
        in_specs=[row(BRANCH_WIDTH), row(BRANCH_WIDTH), row(BRANCH_WIDTH)]
                 + [gate(n, half) for n in range(N_BRANCH) for half in range(2)]
                 + [full(b_gate), once(w_branch_b), once(w_out_b), row(D_MODEL), full(g), full(b), full(w_router_t)],
        out_specs=[row(D_MODEL), row(D_MODEL), pl.BlockSpec((N_EXPERTS, MIX_TM), lambda i: (0, i))],
        out_shape=[jax.ShapeDtypeStruct((SEQ, D_MODEL), F32), jax.ShapeDtypeStruct((SEQ, D_MODEL), F32),
                   jax.ShapeDtypeStruct((N_EXPERTS, SEQ), F32)],
        compiler_params=_params(58, 1),
        name="mix_ln",
    )(ya, yb, yc, rest, rest, rest, rest, rest, rest, b_gate, w_branch_b, w_out_b, x, g, b, w_router_t)


FINAL_TM = 512


def _final_body(y_ref, g_ref, b_ref, xo_ref, xb_ref):
    x2 = _layer_norm(y_ref[...], g_ref[...], b_ref[...])
    xo_ref[...] = x2
    xb_ref[...] = x2.astype(BF16)


def _final_ln(y, g, b):
    row = pl.BlockSpec((FINAL_TM, D_MODEL), lambda i: (i, 0))
    vec = pl.BlockSpec((1, D_MODEL), lambda i: (0, 0))
    return pl.pallas_call(
        _final_body,
        grid=(SEQ // FINAL_TM,),
        in_specs=[row, vec, vec],
        out_specs=[row, row],
        out_shape=[jax.ShapeDtypeStruct((SEQ, D_MODEL), F32), jax.ShapeDtypeStruct((SEQ, D_MODEL), BF16)],
        compiler_params=_params(48, 1),
        name="final_ln",
    )(y, g, b)


FFN_TF = 256
FFN_STEPS = D_FF // FFN_TF
GATHER_STEPS = 4
GATHER_ROWS = CAPACITY // GATHER_STEPS
XROWS_PER_STEP = 168
XROWS_TAIL = CAPACITY - FFN_STEPS * XROWS_PER_STEP
DOWN_HALF = D_MODEL // 2


def _row_copy(x_hbm, rows_ref, sem, src_row, dst_row):
    return pltpu.make_async_copy(x_hbm.at[pl.ds(src_row, 1), :], rows_ref.at[pl.ds(dst_row, 1), :], sem)


ROW_GROUP = 8


def _start_rows(x_hbm, rows_ref, sem, idx_ref, first, count):
    def issue(g, c):
        base = pl.multiple_of(first + g * ROW_GROUP, ROW_GROUP)
        for t in range(ROW_GROUP):
            _row_copy(x_hbm, rows_ref, sem, idx_ref[0, base + t], base + t).start()
        return c
    lax.fori_loop(0, count // ROW_GROUP, issue, 0)


def _wait_rows(x_hbm, rows_ref, sem, count):
    def wait(j, c):
        _row_copy(x_hbm, rows_ref, sem, 0, 0).wait()
        return c
    lax.fori_loop(0, count, wait, 0, unroll=8)


def _ffn_body(l_ref, idx_ref, idx_next_ref, x_hbm, res_hbm, gate_ref, wg_ref, wu_ref, wd_ref, acc_hbm,
              rows_ref, xe_ref, ffn_ref, accb_ref, sem_x, sem_g, sem_s):
    del res_hbm
    e = pl.program_id(0)
    f = pl.program_id(1)
    last = f == FFN_STEPS - 1

    @pl.when((e == 0) & (f == 0))
    def _():
        _start_rows(x_hbm, rows_ref, sem_x, idx_ref, 0, CAPACITY)

    @pl.when(f == 0)
    def _():
        _wait_rows(x_hbm, rows_ref, sem_x, CAPACITY)
        _cast_into(rows_ref, xe_ref, CAPACITY)
        _start_rows(x_hbm, rows_ref, sem_x, idx_next_ref, FFN_STEPS * XROWS_PER_STEP, XROWS_TAIL)

    @pl.when((e > 0) & (f == 1))
    def _():
        _wait_rows(accb_ref, acc_hbm, sem_s, CAPACITY)

    @pl.when((f >= 1) & (f <= GATHER_STEPS))
    def _():
        _start_rows(acc_hbm, accb_ref, sem_g, idx_ref, (f - 1) * GATHER_ROWS, GATHER_ROWS)

    for t in range(XROWS_PER_STEP):
        j = f * XROWS_PER_STEP + t
        _row_copy(x_hbm, rows_ref, sem_x, idx_next_ref[0, j], j).start()

    xe = xe_ref[...]
    a = jnp.dot(xe, wg_ref[...].astype(BF16), preferred_element_type=F32)
    u = jnp.dot(xe, wu_ref[...].astype(BF16), preferred_element_type=F32)
    hid = (jax.nn.silu(a) * u).astype(BF16)

    @pl.when(last)
    def _():
        _wait_rows(acc_hbm, accb_ref, sem_g, CAPACITY)

    for half in range(2):
        cols = slice(half * DOWN_HALF, (half + 1) * DOWN_HALF)
        part = jnp.dot(hid, wd_ref[:, cols].astype(BF16), preferred_element_type=F32)

        @pl.when(f == 0)
        def _():
            ffn_ref[:, cols] = part

        @pl.when((f > 0) & jnp.logical_not(last))
        def _():
            ffn_ref[:, cols] += part

        @pl.when(last)
        def _():
            accb_ref[:, cols] += (ffn_ref[:, cols] + part) * gate_ref[...]

    @pl.when(last)
    def _():
        def issue(g, c):
            base = pl.multiple_of(g * ROW_GROUP, ROW_GROUP)
            for t in range(ROW_GROUP):
                _row_copy(accb_ref, acc_hbm, sem_s, base + t, idx_ref[0, base + t]).start()
            return c
        lax.fori_loop(0, CAPACITY // ROW_GROUP, issue, 0)

    @pl.when(last & (e == N_EXPERTS - 1))
    def _():
        _wait_rows(accb_ref, acc_hbm, sem_s, CAPACITY)
        _wait_rows(x_hbm, rows_ref, sem_x, CAPACITY)


def _expert_ffn_accumulate(l, idx, x1, res, gates, w_gate, w_up, w_down):
    cur = lambda e, f, l_ref: (e, 0, 0)
    nxt = lambda e, f, l_ref: (jnp.minimum(e + 1, N_EXPERTS - 1), 0, 0)
    rows = lambda dt: pltpu.VMEM((CAPACITY, D_MODEL), dt)
    return pl.pallas_call(
        _ffn_body,
        grid_spec=pltpu.PrefetchScalarGridSpec(
            num_scalar_prefetch=1,
            grid=(N_EXPERTS, FFN_STEPS),
            in_specs=[
                pl.BlockSpec((None, 1, CAPACITY), cur, memory_space=pltpu.SMEM),
                pl.BlockSpec((None, 1, CAPACITY), nxt, memory_space=pltpu.SMEM),
                pl.BlockSpec(memory_space=pl.ANY),
                pl.BlockSpec(memory_space=pl.ANY),
                pl.BlockSpec((None, CAPACITY, 1), cur),
                pl.BlockSpec((None, None, D_MODEL, FFN_TF), lambda e, f, l_ref: (l_ref[0], e, 0, f)),
                pl.BlockSpec((None, None, D_MODEL, FFN_TF), lambda e, f, l_ref: (l_ref[0], e, 0, f)),
                pl.BlockSpec((None, None, FFN_TF, D_MODEL), lambda e, f, l_ref: (l_ref[0], e, f, 0)),
            ],
            out_specs=pl.BlockSpec(memory_space=pl.ANY),
            scratch_shapes=[rows(F32), rows(BF16), rows(F32), rows(F32),
                            pltpu.SemaphoreType.DMA, pltpu.SemaphoreType.DMA, pltpu.SemaphoreType.DMA],
        ),
        out_shape=jax.ShapeDtypeStruct((SEQ, D_MODEL), F32),
        input_output_aliases={4: 0},
        compiler_params=_params(58, 2),
        name="expert_ffn",
    )(l, idx, idx, x1, res, gates, w_gate, w_up, w_down)


def kernel(x, w_in, b_gate, rpb, sink, conv_w, w_branch, w_out, ln_g, ln_b, w_router, w_gate, w_up, w_down):
    x0 = x.reshape(SEQ, D_MODEL)
    na_bias = _na_bias_tables(rpb)
    slopes = 2.0 ** (-8.0 * jnp.arange(1, WG_HEADS + 1, dtype=F32) / WG_HEADS)
    head_scalars = jnp.stack([sink, jnp.broadcast_to(slopes, sink.shape)], axis=1)
    w_router_t = jnp.swapaxes(w_router, 1, 2).astype(BF16)

    def layer(carry, per_layer):
        xf, xb = carry
        li, bias_l, hs_l, conv_l, bg_l, g_l, b_l, wr_l = per_layer
        l = li.reshape(1)
        qkv = _project(l, xb, w_in, 0, QKV_WIDTH, BF16)
        rest = _project(l, xb, w_in, QKV_WIDTH, REST_WIDTH, F32)
        ya = _neighbourhood_attention(qkv, bias_l)
        yb = _windowed_gqa(qkv, hs_l)
        yc = _short_conv(rest, conv_l)
        w_branch_b = _cast_layer(l, w_branch.reshape(DEPTH, N_BRANCH * BRANCH_WIDTH, D_MODEL), 512)
        w_out_b = _cast_layer(l, w_out, 512)
        x1, res, aff_t = _mix_ln(ya, yb, yc, rest, bg_l.reshape(1, N_BRANCH * D_MODEL), w_branch_b, w_out_b,
                                 xf, g_l[0:1], b_l[0:1], wr_l)
        gates, idx = lax.top_k(aff_t, CAPACITY)
        y = _expert_ffn_accumulate(l, idx[:, None, :], x1, res, gates[..., None], w_gate, w_up, w_down)
        x2, x2b = _final_ln(y, g_l[1:2], b_l[1:2])
        return (x2, x2b), None

    per_layer = (jnp.arange(DEPTH, dtype=jnp.int32), na_bias, head_scalars, conv_w, b_gate, ln_g, ln_b, w_router_t)
    (xf, _), _ = lax.scan(layer, (x0, _cast_rows(x0, 512)), per_layer)
    return xf.reshape(x.shape)
```

```python
import functools

import jax
import jax.numpy as jnp
from jax import lax
from jax.experimental import pallas as pl
from jax.experimental.pallas import tpu as pltpu

F32 = jnp.float32
BF16 = jnp.bfloat16

D_MODEL = 2048
SEQ = 8192
DEPTH = 4
GRID_W = 64
ROWS = SEQ // GRID_W
HEAD_DIM = 128
NA_HEADS = 8
NA_ROWS = 8
NA_COLS = 16
WG_HEADS = 8
WG_KV_HEADS = 2
WG_GROUP = WG_HEADS // WG_KV_HEADS
WG_WINDOW = 128
WG_BLOCK = 128
SC_WIDTH = 1024
BRANCH_WIDTH = 1024
N_BRANCH = 3
N_EXPERTS = 16
CAPACITY = 2 * SEQ // N_EXPERTS
D_FF = 1536
ALPHA = (2 * DEPTH) ** 0.25
LN_EPS = 1e-5
ATTN_SCALE = HEAD_DIM ** -0.5
MASKED = -1e30

QKV_WIDTH = 3 * NA_HEADS * HEAD_DIM + (WG_HEADS + 2 * WG_KV_HEADS) * HEAD_DIM
REST_WIDTH = 3 * SC_WIDTH + N_BRANCH * D_MODEL
NA_Q_BLK, NA_K_BLK, NA_V_BLK = 0, NA_HEADS, 2 * NA_HEADS
WG_Q_BLK = 3 * NA_HEADS * HEAD_DIM // (WG_GROUP * HEAD_DIM)
WG_K_BLK = (3 * NA_HEADS + WG_HEADS) * HEAD_DIM // HEAD_DIM
WG_V_BLK = WG_K_BLK + WG_KV_HEADS
GATE_COL = 3 * SC_WIDTH

MIB = 1024 * 1024


def _params(vmem_mib, n_axes):
    return pltpu.CompilerParams(dimension_semantics=("arbitrary",) * n_axes,
                                vmem_limit_bytes=vmem_mib * MIB)


def _cast_body(x_ref, o_ref):
    o_ref[...] = x_ref[...].astype(BF16)


def _cast_rows(x, tm):
    m, n = x.shape
    return pl.pallas_call(
        _cast_body,
        grid=(m // tm,),
        in_specs=[pl.BlockSpec((tm, n), lambda i: (i, 0))],
        out_specs=pl.BlockSpec((tm, n), lambda i: (i, 0)),
        out_shape=jax.ShapeDtypeStruct((m, n), BF16),
        compiler_params=_params(32, 1),
        name="cast_rows",
    )(x)


def _cast_layer_body(l_ref, x_ref, o_ref):
    o_ref[...] = x_ref[...].astype(BF16)


def _cast_layer(l, w, tm):
    _, m, n = w.shape
    return pl.pallas_call(
        _cast_layer_body,
        grid_spec=pltpu.PrefetchScalarGridSpec(
            num_scalar_prefetch=1,
            grid=(m // tm,),
            in_specs=[pl.BlockSpec((None, tm, n), lambda i, l_ref: (l_ref[0], i, 0))],
            out_specs=pl.BlockSpec((tm, n), lambda i, l_ref: (i, 0)),
        ),
        out_shape=jax.ShapeDtypeStruct((m, n), BF16),
        compiler_params=_params(32, 1),
        name="cast_layer",
    )(l, w)


PROJ_TM = 1024
PROJ_TN = 1536
CAST_ROWS = 256


def _cast_into(w_ref, wb_ref, rows):
    def chunk(i, c):
        sl = pl.ds(pl.multiple_of(i * CAST_ROWS, CAST_ROWS), CAST_ROWS)
        wb_ref[sl, :] = w_ref[sl, :].astype(BF16)
        return c
    lax.fori_loop(0, rows // CAST_ROWS, chunk, 0)


def _proj_body(l_ref, x_ref, w_ref, o_ref, wb_ref):
    @pl.when(pl.program_id(1) == 0)
    def _():
        _cast_into(w_ref, wb_ref, D_MODEL)

    o_ref[...] = jnp.dot(x_ref[...], wb_ref[...], preferred_element_type=F32).astype(o_ref.dtype)


def _project(l, xb, w_in, col0, width, out_dtype):
    n_tiles, off = width // PROJ_TN, col0 // PROJ_TN
    return pl.pallas_call(
        _proj_body,
        grid_spec=pltpu.PrefetchScalarGridSpec(
            num_scalar_prefetch=1,
            grid=(n_tiles, SEQ // PROJ_TM),
            in_specs=[
                pl.BlockSpec((PROJ_TM, D_MODEL), lambda j, i, l_ref: (i, 0)),
                pl.BlockSpec((None, D_MODEL, PROJ_TN), lambda j, i, l_ref: (l_ref[0], 0, off + j)),
            ],
            out_specs=pl.BlockSpec((PROJ_TM, PROJ_TN), lambda j, i, l_ref: (i, j)),
            scratch_shapes=[pltpu.VMEM((D_MODEL, PROJ_TN), BF16)],
        ),
        out_shape=jax.ShapeDtypeStruct((SEQ, width), out_dtype),
        compiler_params=_params(58, 2),
        name="in_proj",
    )(l, xb, w_in)


NA_CHUNK_ROWS = 32
NA_GROUP = 8
NA_KEYS = NA_ROWS * GRID_W


def _na_bias_tables(rpb):
    c = jnp.arange(GRID_W)[:, None]
    cj = jnp.arange(GRID_W)[None, :]
    cs = jnp.clip(c - NA_COLS // 2, 0, GRID_W - NA_COLS)
    valid = (cj >= cs) & (cj < cs + NA_COLS)
    select = ((cj - c + (NA_COLS - 1))[None] == jnp.arange(2 * NA_COLS - 1)[:, None, None]) & valid[None]
    t = jnp.einsum('lhrk,kcj->lhcrj', rpb.astype(F32), select.astype(F32),
                   precision=lax.Precision.HIGHEST)
    t = jnp.where(valid[None, None, :, None, :], t, MASKED)
    per_d = [t[:, :, :, NA_ROWS - 1 - d:2 * NA_ROWS - 1 - d, :].reshape(DEPTH, NA_HEADS, GRID_W, NA_KEYS)
             for d in range(NA_ROWS)]
    return jnp.stack(per_d, axis=1)


def _na_body(q_ref, k_ref, v_ref, bias_ref, o_ref):
    chunk = pl.program_id(1)

    def group(gi, carry):
        qss, kss, logits = [], [], []
        for t in range(NA_GROUP):
            i = gi * NA_GROUP + t
            r = chunk * NA_CHUNK_ROWS + i
            rs = jnp.clip(r - NA_ROWS // 2, 0, ROWS - NA_ROWS)
            qs = pl.ds(pl.multiple_of(i * GRID_W, GRID_W), GRID_W)
            ks = pl.ds(pl.multiple_of(rs * GRID_W, GRID_W), NA_KEYS)
            s = lax.dot_general(q_ref[qs, :], k_ref[ks, :], (((1,), (1,)), ((), ())),
                                preferred_element_type=F32)
            logits.append(s * ATTN_SCALE + bias_ref[r - rs])
            qss.append(qs)
            kss.append(ks)
        probs, dens = [], []
        for s in logits:
            e = jnp.exp(s - jnp.max(s, axis=-1, keepdims=True))
            dens.append(jnp.sum(e, axis=-1, keepdims=True))
            probs.append(e.astype(BF16))
        for qs, ks, p, den in zip(qss, kss, probs, dens):
            o = jnp.dot(p, v_ref[ks, :], preferred_element_type=F32) / den
            o_ref[qs, :] = o.astype(o_ref.dtype)
        return carry

    lax.fori_loop(0, NA_CHUNK_ROWS // NA_GROUP, group, 0)


def _neighbourhood_attention(qkv, bias):
    tq = NA_CHUNK_ROWS * GRID_W
    return pl.pallas_call(
        _na_body,
        grid=(NA_HEADS, ROWS // NA_CHUNK_ROWS),
        in_specs=[
            pl.BlockSpec((tq, HEAD_DIM), lambda h, c: (c, NA_Q_BLK + h)),
            pl.BlockSpec((SEQ, HEAD_DIM), lambda h, c: (0, NA_K_BLK + h)),
            pl.BlockSpec((SEQ, HEAD_DIM), lambda h, c: (0, NA_V_BLK + h)),
            pl.BlockSpec((NA_ROWS, None, GRID_W, NA_KEYS), lambda h, c: (0, h, 0, 0)),
        ],
        out_specs=pl.BlockSpec((tq, HEAD_DIM), lambda h, c: (c, h)),
        out_shape=jax.ShapeDtypeStruct((SEQ, NA_HEADS * HEAD_DIM), BF16),
        compiler_params=_params(32, 2),
        name="na_attn",
    )(qkv, qkv, qkv, bias)


WG_CHUNK_BLOCKS = 8
WG_KEYS = 3 * WG_BLOCK


def _wg_body(hs_ref, q_ref, k_ref, v_ref, o_ref):
    kv = pl.program_id(0)
    chunk = pl.program_id(1)

    def block(j, carry):
        n = chunk * WG_CHUNK_BLOCKS + j
        start = jnp.clip((n - 1) * WG_BLOCK, 0, SEQ - WG_KEYS)
        ks = pl.ds(pl.multiple_of(start, WG_BLOCK), WG_KEYS)
        qs = pl.ds(pl.multiple_of(j * WG_BLOCK, WG_BLOCK), WG_BLOCK)
        kw = k_ref[ks, :]
        vw = v_ref[ks, :]
        q_pos = n * WG_BLOCK + lax.broadcasted_iota(jnp.int32, (WG_BLOCK, WG_KEYS), 0)
        k_pos = start + lax.broadcasted_iota(jnp.int32, (WG_BLOCK, WG_KEYS), 1)
        dist = jnp.abs(k_pos - q_pos)
        valid = dist <= WG_WINDOW
        dist_f = dist.astype(F32)
        heads = [(slice(g * HEAD_DIM, (g + 1) * HEAD_DIM), hs_ref[0, kv * WG_GROUP + g], hs_ref[1, kv * WG_GROUP + g])
                 for g in range(WG_GROUP)]
        logits = []
        for cols, sink, slope in heads:
            s = lax.dot_general(q_ref[qs, cols], kw, (((1,), (1,)), ((), ())), preferred_element_type=F32)
            logits.append(jnp.where(valid, s * ATTN_SCALE - slope * dist_f, MASKED))
        probs, dens = [], []
        for s, (cols, sink, slope) in zip(logits, heads):
            m = jnp.maximum(jnp.max(s, axis=-1, keepdims=True), sink)
            e = jnp.exp(s - m)
            dens.append(jnp.sum(e, axis=-1, keepdims=True) + jnp.exp(sink - m))
            probs.append(e.astype(BF16))
        for p, den, (cols, sink, slope) in zip(probs, dens, heads):
            o = jnp.dot(p, vw, preferred_element_type=F32) / den
            o_ref[qs, cols] = o.astype(o_ref.dtype)
        return carry

    lax.fori_loop(0, WG_CHUNK_BLOCKS, block, 0)


def _windowed_gqa(qkv, head_scalars):
    tq = WG_CHUNK_BLOCKS * WG_BLOCK
    gw = WG_GROUP * HEAD_DIM
    return pl.pallas_call(
        _wg_body,
        grid=(WG_KV_HEADS, SEQ // tq),
        in_specs=[
            pl.BlockSpec(memory_space=pltpu.SMEM),
            pl.BlockSpec((tq, gw), lambda k, c: (c, WG_Q_BLK + k)),
            pl.BlockSpec((SEQ, HEAD_DIM), lambda k, c: (0, WG_K_BLK + k)),
            pl.BlockSpec((SEQ, HEAD_DIM), lambda k, c: (0, WG_V_BLK + k)),
        ],
        out_specs=pl.BlockSpec((tq, gw), lambda k, c: (c, k)),
        out_shape=jax.ShapeDtypeStruct((SEQ, WG_HEADS * HEAD_DIM), BF16),
        compiler_params=_params(32, 2),
        name="wg_attn",
    )(head_scalars, qkv, qkv, qkv)


CONV_TM = 512
HALO = 8


def _conv_body(bg_ref, cg_ref, hc_ref, cgp_ref, hcp_ref, cgn_ref, hcn_ref, w_ref, o_ref):
    i = pl.program_id(0)
    u = cg_ref[...] * hc_ref[...]
    prev_row = cgp_ref[HALO - 1:HALO, :] * hcp_ref[HALO - 1:HALO, :]
    next_row = cgn_ref[0:1, :] * hcn_ref[0:1, :]
    prev_row = jnp.where(i == 0, 0.0, prev_row)
    next_row = jnp.where(i == pl.num_programs(0) - 1, 0.0, next_row)
    row = lax.broadcasted_iota(jnp.int32, u.shape, 0)
    u_prev = jnp.where(row == 0, prev_row, pltpu.roll(u, 1, 0))
    u_next = jnp.where(row == CONV_TM - 1, next_row, pltpu.roll(u, CONV_TM - 1, 0))
    y = bg_ref[...] * (w_ref[0:1, :] * u_prev + w_ref[1:2, :] * u + w_ref[2:3, :] * u_next)
    o_ref[...] = y.astype(o_ref.dtype)


def _short_conv(rest, conv_w):
    nb = CONV_TM // HALO
    last = SEQ // HALO - 1
    main = lambda col: pl.BlockSpec((CONV_TM, SC_WIDTH), lambda i: (i, col))
    prev = lambda col: pl.BlockSpec((HALO, SC_WIDTH), lambda i: (jnp.maximum(i * nb - 1, 0), col))
    nxt = lambda col: pl.BlockSpec((HALO, SC_WIDTH), lambda i: (jnp.minimum((i + 1) * nb, last), col))
    return pl.pallas_call(
        _conv_body,
        grid=(SEQ // CONV_TM,),
        in_specs=[main(0), main(1), main(2), prev(1), prev(2), nxt(1), nxt(2),
                  pl.BlockSpec((3, SC_WIDTH), lambda i: (0, 0))],
        out_specs=pl.BlockSpec((CONV_TM, SC_WIDTH), lambda i: (i, 0)),
        out_shape=jax.ShapeDtypeStruct((SEQ, SC_WIDTH), BF16),
        compiler_params=_params(40, 1),
        name="short_conv",
    )(rest, rest, rest, rest, rest, rest, rest, conv_w)


MERGE_TM = 512
MERGE_TN = 512


def _merge_body(l_ref, ya_ref, yb_ref, yc_ref, g0_ref, g1_ref, g2_ref, b0_ref, b1_ref, b2_ref, w_ref,
                o_ref, wb_ref):
    @pl.when(pl.program_id(1) == 0)
    def _():
        for n in range(N_BRANCH):
            _cast_into(w_ref.at[n], wb_ref.at[n], BRANCH_WIDTH)

    acc = jnp.zeros((MERGE_TM, MERGE_TN), F32)
    for n, (y_ref, g_ref, b_ref) in enumerate(((ya_ref, g0_ref, b0_ref), (yb_ref, g1_ref, b1_ref),
                                               (yc_ref, g2_ref, b2_ref))):
        branch = jnp.dot(y_ref[...], wb_ref[n], preferred_element_type=F32)
        acc = acc + jax.nn.sigmoid(g_ref[...] + b_ref[...]) * branch
    o_ref[...] = acc.astype(o_ref.dtype)


def _merge(l, ya, yb, yc, rest, b_gate, w_branch):
    y_spec = pl.BlockSpec((MERGE_TM, BRANCH_WIDTH), lambda j, i, l_ref: (i, 0))
    gcol = lambda n: (GATE_COL + n * D_MODEL) // MERGE_TN
    g_spec = lambda n: pl.BlockSpec((MERGE_TM, MERGE_TN), lambda j, i, l_ref: (i, gcol(n) + j))
    b_spec = lambda n: pl.BlockSpec((1, MERGE_TN), lambda j, i, l_ref: (0, n * D_MODEL // MERGE_TN + j))
    return pl.pallas_call(
        _merge_body,
        grid_spec=pltpu.PrefetchScalarGridSpec(
            num_scalar_prefetch=1,
            grid=(D_MODEL // MERGE_TN, SEQ // MERGE_TM),
            in_specs=[y_spec, y_spec, y_spec, g_spec(0), g_spec(1), g_spec(2), b_spec(0), b_spec(1), b_spec(2),
                      pl.BlockSpec((None, N_BRANCH, BRANCH_WIDTH, MERGE_TN),
                                   lambda j, i, l_ref: (l_ref[0], 0, 0, j))],
            out_specs=pl.BlockSpec((MERGE_TM, MERGE_TN), lambda j, i, l_ref: (i, j)),
            scratch_shapes=[pltpu.VMEM((N_BRANCH, BRANCH_WIDTH, MERGE_TN), BF16)],
        ),
        out_shape=jax.ShapeDtypeStruct((SEQ, D_MODEL), BF16),
        compiler_params=_params(48, 2),
        name="branch_merge",
    )(l, ya, yb, yc, rest, rest, rest, b_gate, b_gate, b_gate, w_branch)


def _layer_norm(y, g, b):
    mu = jnp.mean(y, axis=-1, keepdims=True)
    yc = y - mu
    var = jnp.mean(yc * yc, axis=-1, keepdims=True)
    return yc * lax.rsqrt(var + LN_EPS) * g + b


OUT_TM = 256


def _out_body(m_ref, w_ref, x_ref, g_ref, b_ref, wr_ref, xo_ref, res_ref, aff_ref):
    mix = jnp.dot(m_ref[...], w_ref[...], preferred_element_type=F32)
    x1 = _layer_norm(ALPHA * x_ref[...] + mix, g_ref[...], b_ref[...])
    xo_ref[...] = x1
    res_ref[...] = ALPHA * x1
    logits = lax.dot_general(wr_ref[...], x1.astype(BF16), (((1,), (1,)), ((), ())),
                             preferred_element_type=F32)
    e = jnp.exp(logits - jnp.max(logits, axis=0, keepdims=True))
    aff_ref[...] = e / jnp.sum(e, axis=0, keepdims=True)


def _out_proj_ln(merged, w_out_b, x, g, b, w_router_t):
    row = pl.BlockSpec((OUT_TM, D_MODEL), lambda i: (i, 0))
    full = lambda a: pl.BlockSpec(a.shape, lambda i: (0,) * a.ndim)
    return pl.pallas_call(
        _out_body,
        grid=(SEQ // OUT_TM,),
        in_specs=[row, full(w_out_b), row, full(g), full(b), full(w_router_t)],
        out_specs=[row, row, pl.BlockSpec((N_EXPERTS, OUT_TM), lambda i: (0, i))],
        out_shape=[jax.ShapeDtypeStruct((SEQ, D_MODEL), F32), jax.ShapeDtypeStruct((SEQ, D_MODEL), F32),
                   jax.ShapeDtypeStruct((N_EXPERTS, SEQ), F32)],
        compiler_params=_params(48, 1),
        name="out_proj_ln",
    )(merged, w_out_b, x, g, b, w_router_t)


MIX_TM = 256
GATE_TN = 1024


def _mix_body(ya_ref, yb_ref, yc_ref, g00, g01, g10, g11, g20, g21, bg_ref, wb_ref, wo_ref, x_ref, g_ref, b_ref,
              wr_ref, xo_ref, res_ref, aff_ref):
    y_refs = (ya_ref, yb_ref, yc_ref)
    gate_refs = ((g00, g01), (g10, g11), (g20, g21))
    halves = []
    for half in range(D_MODEL // GATE_TN):
        acc = jnp.zeros((MIX_TM, GATE_TN), F32)
        for n in range(N_BRANCH):
            col0 = n * D_MODEL + half * GATE_TN
            branch = jnp.dot(y_refs[n][...],
                             wb_ref[n * BRANCH_WIDTH:(n + 1) * BRANCH_WIDTH, half * GATE_TN:(half + 1) * GATE_TN],
                             preferred_element_type=F32)
            gate = jax.nn.sigmoid(gate_refs[n][half][...] + bg_ref[:, col0:col0 + GATE_TN])
            acc = acc + gate * branch
        halves.append(acc.astype(BF16))
    mix = sum(jnp.dot(m, wo_ref[h * GATE_TN:(h + 1) * GATE_TN, :], preferred_element_type=F32)
              for h, m in enumerate(halves))
    x1 = _layer_norm(ALPHA * x_ref[...] + mix, g_ref[...], b_ref[...])
    xo_ref[...] = x1
    res_ref[...] = ALPHA * x1
    logits = lax.dot_general(wr_ref[...], x1.astype(BF16), (((1,), (1,)), ((), ())),
                             preferred_element_type=F32)
    e = jnp.exp(logits - jnp.max(logits, axis=0, keepdims=True))
    aff_ref[...] = e / jnp.sum(e, axis=0, keepdims=True)


def _mix_ln(ya, yb, yc, rest, b_gate, w_branch_b, w_out_b, x, g, b, w_router_t):
    row = lambda n: pl.BlockSpec((MIX_TM, n), lambda i: (i, 0))
    gate = lambda n, half: pl.BlockSpec(
        (MIX_TM, GATE_TN), lambda i: (i, (GATE_COL + n * D_MODEL) // GATE_TN + half))
    full = lambda a: pl.BlockSpec(a.shape, lambda i: (0,) * a.ndim)
    once = lambda a: pl.BlockSpec(a.shape, lambda i: (0,) * a.ndim, pipeline_mode=pl.Buffered(1))
    return pl.pallas_call(
        _mix_body,
        grid=(SEQ // MIX_TM,),
        in_specs=[row(BRANCH_WIDTH), row(BRANCH_WIDTH), row(BRANCH_WIDTH)]
                 + [gate(n, half) for n in range(N_BRANCH) for half in range(2)]
                 + [full(b_gate), once(w_branch_b), once(w_out_b), row(D_MODEL), full(g), full(b), full(w_router_t)],
        out_specs=[row(D_MODEL), row(D_MODEL), pl.BlockSpec((N_EXPERTS, MIX_TM), lambda i: (0, i))],
        out_shape=[jax.ShapeDtypeStruct((SEQ, D_MODEL), F32), jax.ShapeDtypeStruct((SEQ, D_MODEL), F32),
                   jax.ShapeDtypeStruct((N_EXPERTS, SEQ), F32)],
        compiler_params=_params(58, 1),
        name="mix_ln",
    )(ya, yb, yc, rest, rest, rest, rest, rest, rest, b_gate, w_branch_b, w_out_b, x, g, b, w_router_t)


FINAL_TM = 512


def _final_body(y_ref, g_ref, b_ref, xo_ref, xb_ref):
    x2 = _layer_norm(y_ref[...], g_ref[...], b_ref[...])
    xo_ref[...] = x2
    xb_ref[...] = x2.astype(BF16)


def _final_ln(y, g, b):
    row = pl.BlockSpec((FINAL_TM, D_MODEL), lambda i: (i, 0))
    vec = pl.BlockSpec((1, D_MODEL), lambda i: (0, 0))
    return pl.pallas_call(
        _final_body,
        grid=(SEQ // FINAL_TM,),
        in_specs=[row, vec, vec],
        out_specs=[row, row],
        out_shape=[jax.ShapeDtypeStruct((SEQ, D_MODEL), F32), jax.ShapeDtypeStruct((SEQ, D_MODEL), BF16)],
        compiler_params=_params(48, 1),
        name="final_ln",
    )(y, g, b)


FFN_TF = 256
FFN_STEPS = D_FF // FFN_TF
GATHER_STEPS = 4
GATHER_ROWS = CAPACITY // GATHER_STEPS
XROWS_PER_STEP = 168
XROWS_TAIL = CAPACITY - FFN_STEPS * XROWS_PER_STEP
DOWN_HALF = D_MODEL // 2


def _row_copy(x_hbm, rows_ref, sem, src_row, dst_row):
    return pltpu.make_async_copy(x_hbm.at[pl.ds(src_row, 1), :], rows_ref.at[pl.ds(dst_row, 1), :], sem)


ROW_GROUP = 8


def _start_rows(x_hbm, rows_ref, sem, idx_ref, first, count):
    def issue(g, c):
        base = pl.multiple_of(first + g * ROW_GROUP, ROW_GROUP)
        for t in range(ROW_GROUP):
            _row_copy(x_hbm, rows_ref, sem, idx_ref[0, base + t], base + t).start()
        return c
    lax.fori_loop(0, count // ROW_GROUP, issue, 0)


def _wait_rows(x_hbm, rows_ref, sem, count):
    def wait(j, c):
        _row_copy(x_hbm, rows_ref, sem, 0, 0).wait()
        return c
    lax.fori_loop(0, count, wait, 0, unroll=8)


def _ffn_body(l_ref, idx_ref, idx_next_ref, x_hbm, res_hbm, gate_ref, wg_ref, wu_ref, wd_ref, acc_hbm,
              rows_ref, xe_ref, ffn_ref, accb_ref, sem_x, sem_g, sem_s):
    del res_hbm
    e = pl.program_id(0)
    f = pl.program_id(1)
    last = f == FFN_STEPS - 1

    @pl.when((e == 0) & (f == 0))
    def _():
        _start_rows(x_hbm, rows_ref, sem_x, idx_ref, 0, CAPACITY)

    @pl.when(f == 0)
    def _():
        _wait_rows(x_hbm, rows_ref, sem_x, CAPACITY)
        _cast_into(rows_ref, xe_ref, CAPACITY)
        _start_rows(x_hbm, rows_ref, sem_x, idx_next_ref, FFN_STEPS * XROWS_PER_STEP, XROWS_TAIL)

    @pl.when((e > 0) & (f == 1))
    def _():
        _wait_rows(accb_ref, acc_hbm, sem_s, CAPACITY)

    @pl.when((f >= 1) & (f <= GATHER_STEPS))
    def _():
        _start_rows(acc_hbm, accb_ref, sem_g, idx_ref, (f - 1) * GATHER_ROWS, GATHER_ROWS)

    xe = xe_ref[...]
    a = jnp.dot(xe, wg_ref[...].astype(BF16), preferred_element_type=F32)
    for t in range(XROWS_PER_STEP):
        j = f * XROWS_PER_STEP + t
        _row_copy(x_hbm, rows_ref, sem_x, idx_next_ref[0, j], j).start()
    u = jnp.dot(xe, wu_ref[...].astype(BF16), preferred_element_type=F32)
    hid = (jax.nn.silu(a) * u).astype(BF16)

    @pl.when(last)
    def _():
        _wait_rows(acc_hbm, accb_ref, sem_g, CAPACITY)

    for half in range(2):
        cols = slice(half * DOWN_HALF, (half + 1) * DOWN_HALF)
        part = jnp.dot(hid, wd_ref[:, cols].astype(BF16), preferred_element_type=F32)

        @pl.when(f == 0)
        def _():
            ffn_ref[:, cols] = part

        @pl.when((f > 0) & jnp.logical_not(last))
        def _():
            ffn_ref[:, cols] += part

        @pl.when(last)
        def _():
            accb_ref[:, cols] += (ffn_ref[:, cols] + part) * gate_ref[...]

    @pl.when(last)
    def _():
        def issue(g, c):
            base = pl.multiple_of(g * ROW_GROUP, ROW_GROUP)
            for t in range(ROW_GROUP):
                _row_copy(accb_ref, acc_hbm, sem_s, base + t, idx_ref[0, base + t]).start()
            return c
        lax.fori_loop(0, CAPACITY // ROW_GROUP, issue, 0)

    @pl.when(last & (e == N_EXPERTS - 1))
    def _():
        _wait_rows(accb_ref, acc_hbm, sem_s, CAPACITY)
        _wait_rows(x_hbm, rows_ref, sem_x, CAPACITY)


def _expert_ffn_accumulate(l, idx, x1, res, gates, w_gate, w_up, w_down):
    cur = lambda e, f, l_ref: (e, 0, 0)
    nxt = lambda e, f, l_ref: (jnp.minimum(e + 1, N_EXPERTS - 1), 0, 0)
    rows = lambda dt: pltpu.VMEM((CAPACITY, D_MODEL), dt)
    return pl.pallas_call(
        _ffn_body,
        grid_spec=pltpu.PrefetchScalarGridSpec(
            num_scalar_prefetch=1,
            grid=(N_EXPERTS, FFN_STEPS),
            in_specs=[
                pl.BlockSpec((None, 1, CAPACITY), cur, memory_space=pltpu.SMEM),
                pl.BlockSpec((None, 1, CAPACITY), nxt, memory_space=pltpu.SMEM),
                pl.BlockSpec(memory_space=pl.ANY),
                pl.BlockSpec(memory_space=pl.ANY),
                pl.BlockSpec((None, CAPACITY, 1), cur),
                pl.BlockSpec((None, None, D_MODEL, FFN_TF), lambda e, f, l_ref: (l_ref[0], e, 0, f)),
                pl.BlockSpec((None, None, D_MODEL, FFN_TF), lambda e, f, l_ref: (l_ref[0], e, 0, f)),
                pl.BlockSpec((None, None, FFN_TF, D_MODEL), lambda e, f, l_ref: (l_ref[0], e, f, 0)),
            ],
            out_specs=pl.BlockSpec(memory_space=pl.ANY),
            scratch_shapes=[rows(F32), rows(BF16), rows(F32), rows(F32),
                            pltpu.SemaphoreType.DMA, pltpu.SemaphoreType.DMA, pltpu.SemaphoreType.DMA],
        ),
        out_shape=jax.ShapeDtypeStruct((SEQ, D_MODEL), F32),
        input_output_aliases={4: 0},
        compiler_params=_params(58, 2),
        name="expert_ffn",
    )(l, idx, idx, x1, res, gates, w_gate, w_up, w_down)


def kernel(x, w_in, b_gate, rpb, sink, conv_w, w_branch, w_out, ln_g, ln_b, w_router, w_gate, w_up, w_down):
    x0 = x.reshape(SEQ, D_MODEL)
    na_bias = _na_bias_tables(rpb)
    slopes = 2.0 ** (-8.0 * jnp.arange(1, WG_HEADS + 1, dtype=F32) / WG_HEADS)
    head_scalars = jnp.stack([sink, jnp.broadcast_to(slopes, sink.shape)], axis=1)
    w_router_t = jnp.swapaxes(w_router, 1, 2).astype(BF16)

    def layer(carry, per_layer):
        xf, xb = carry
        li, bias_l, hs_l, conv_l, bg_l, g_l, b_l, wr_l = per_layer
        l = li.reshape(1)
        qkv = _project(l, xb, w_in, 0, QKV_WIDTH, BF16)
        rest = _project(l, xb, w_in, QKV_WIDTH, REST_WIDTH, F32)
        ya = _neighbourhood_attention(qkv, bias_l)
        yb = _windowed_gqa(qkv, hs_l)
        yc = _short_conv(rest, conv_l)
        w_branch_b = _cast_layer(l, w_branch.reshape(DEPTH, N_BRANCH * BRANCH_WIDTH, D_MODEL), 512)
        w_out_b = _cast_layer(l, w_out, 512)
        x1, res, aff_t = _mix_ln(ya, yb, yc, rest, bg_l.reshape(1, N_BRANCH * D_MODEL), w_branch_b, w_out_b,
                                 xf, g_l[0:1], b_l[0:1], wr_l)
        gates, idx = lax.top_k(aff_t, CAPACITY)
        y = _expert_ffn_accumulate(l, idx[:, None, :], x1, res, gates[..., None], w_gate, w_up, w_down)
        x2, x2b = _final_ln(y, g_l[1:2], b_l[1:2])
        return (x2, x2b), None

    per_layer = (jnp.arange(DEPTH, dtype=jnp.int32), na_bias, head_scalars, conv_w, b_gate, ln_g, ln_b, w_router_t)
    (xf, _), _ = lax.scan(layer, (x0, _cast_rows(x0, 512)), per_layer)
    return xf.reshape(x.shape)
```

```python
import functools

import jax
import jax.numpy as jnp
from jax import lax
from jax.experimental import pallas as pl
from jax.experimental.pallas import tpu as pltpu

F32 = jnp.float32
BF16 = jnp.bfloat16

D_MODEL = 2048
SEQ = 8192
DEPTH = 4
GRID_W = 64
ROWS = SEQ // GRID_W
HEAD_DIM = 128
NA_HEADS = 8
NA_ROWS = 8
NA_COLS = 16
WG_HEADS = 8
WG_KV_HEADS = 2
WG_GROUP = WG_HEADS // WG_KV_HEADS
WG_WINDOW = 128
WG_BLOCK = 128
SC_WIDTH = 1024
BRANCH_WIDTH = 1024
N_BRANCH = 3
N_EXPERTS = 16
CAPACITY = 2 * SEQ // N_EXPERTS
D_FF = 1536
ALPHA = (2 * DEPTH) ** 0.25
LN_EPS = 1e-5
ATTN_SCALE = HEAD_DIM ** -0.5
MASKED = -1e30

QKV_WIDTH = 3 * NA_HEADS * HEAD_DIM + (WG_HEADS + 2 * WG_KV_HEADS) * HEAD_DIM
REST_WIDTH = 3 * SC_WIDTH + N_BRANCH * D_MODEL
NA_Q_BLK, NA_K_BLK, NA_V_BLK = 0, NA_HEADS, 2 * NA_HEADS
WG_Q_BLK = 3 * NA_HEADS * HEAD_DIM // (WG_GROUP * HEAD_DIM)
WG_K_BLK = (3 * NA_HEADS + WG_HEADS) * HEAD_DIM // HEAD_DIM
WG_V_BLK = WG_K_BLK + WG_KV_HEADS
GATE_COL = 3 * SC_WIDTH

MIB = 1024 * 1024


def _params(vmem_mib, n_axes):
    return pltpu.CompilerParams(dimension_semantics=("arbitrary",) * n_axes,
                                vmem_limit_bytes=vmem_mib * MIB)


def _cast_body(x_ref, o_ref):
    o_ref[...] = x_ref[...].astype(BF16)


def _cast_rows(x, tm):
    m, n = x.shape
    return pl.pallas_call(
        _cast_body,
        grid=(m // tm,),
        in_specs=[pl.BlockSpec((tm, n), lambda i: (i, 0))],
        out_specs=pl.BlockSpec((tm, n), lambda i: (i, 0)),
        out_shape=jax.ShapeDtypeStruct((m, n), BF16),
        compiler_params=_params(32, 1),
        name="cast_rows",
    )(x)


def _cast_layer_body(l_ref, x_ref, o_ref):
    o_ref[...] = x_ref[...].astype(BF16)


def _cast_layer(l, w, tm):
    _, m, n = w.shape
    return pl.pallas_call(
        _cast_layer_body,
        grid_spec=pltpu.PrefetchScalarGridSpec(
            num_scalar_prefetch=1,
            grid=(m // tm,),
            in_specs=[pl.BlockSpec((None, tm, n), lambda i, l_ref: (l_ref[0], i, 0))],
            out_specs=pl.BlockSpec((tm, n), lambda i, l_ref: (i, 0)),
        ),
        out_shape=jax.ShapeDtypeStruct((m, n), BF16),
        compiler_params=_params(32, 1),
        name="cast_layer",
    )(l, w)


PROJ_TM = 1024
PROJ_TN = 1536
CAST_ROWS = 256


def _cast_into(w_ref, wb_ref, rows):
    def chunk(i, c):
        sl = pl.ds(pl.multiple_of(i * CAST_ROWS, CAST_ROWS), CAST_ROWS)
        wb_ref[sl, :] = w_ref[sl, :].astype(BF16)
        return c
    lax.fori_loop(0, rows // CAST_ROWS, chunk, 0)


def _proj_body(l_ref, x_ref, w_ref, o_ref, wb_ref):
    @pl.when(pl.program_id(1) == 0)
    def _():
        _cast_into(w_ref, wb_ref, D_MODEL)

    o_ref[...] = jnp.dot(x_ref[...], wb_ref[...], preferred_element_type=F32).astype(o_ref.dtype)


def _project(l, xb, w_in, col0, width, out_dtype):
    n_tiles, off = width // PROJ_TN, col0 // PROJ_TN
    return pl.pallas_call(
        _proj_body,
        grid_spec=pltpu.PrefetchScalarGridSpec(
            num_scalar_prefetch=1,
            grid=(n_tiles, SEQ // PROJ_TM),
            in_specs=[
                pl.BlockSpec((PROJ_TM, D_MODEL), lambda j, i, l_ref: (i, 0)),
                pl.BlockSpec((None, D_MODEL, PROJ_TN), lambda j, i, l_ref: (l_ref[0], 0, off + j)),
            ],
            out_specs=pl.BlockSpec((PROJ_TM, PROJ_TN), lambda j, i, l_ref: (i, j)),
            scratch_shapes=[pltpu.VMEM((D_MODEL, PROJ_TN), BF16)],
        ),
        out_shape=jax.ShapeDtypeStruct((SEQ, width), out_dtype),
        compiler_params=_params(58, 2),
        name="in_proj",
    )(l, xb, w_in)


NA_CHUNK_ROWS = 32
NA_GROUP = 8
NA_KEYS = NA_ROWS * GRID_W


def _na_bias_tables(rpb):
    c = jnp.arange(GRID_W)[:, None]
    cj = jnp.arange(GRID_W)[None, :]
    cs = jnp.clip(c - NA_COLS // 2, 0, GRID_W - NA_COLS)
    valid = (cj >= cs) & (cj < cs + NA_COLS)
    select = ((cj - c + (NA_COLS - 1))[None] == jnp.arange(2 * NA_COLS - 1)[:, None, None]) & valid[None]
    t = jnp.einsum('lhrk,kcj->lhcrj', rpb.astype(F32), select.astype(F32),
                   precision=lax.Precision.HIGHEST)
    t = jnp.where(valid[None, None, :, None, :], t, MASKED)
    per_d = [t[:, :, :, NA_ROWS - 1 - d:2 * NA_ROWS - 1 - d, :].reshape(DEPTH, NA_HEADS, GRID_W, NA_KEYS)
             for d in range(NA_ROWS)]
    return jnp.stack(per_d, axis=1)


def _na_body(q_ref, k_ref, v_ref, bias_ref, o_ref):
    chunk = pl.program_id(1)

    def group(gi, carry):
        qss, kss, logits = [], [], []
        for t in range(NA_GROUP):
            i = gi * NA_GROUP + t
            r = chunk * NA_CHUNK_ROWS + i
            rs = jnp.clip(r - NA_ROWS // 2, 0, ROWS - NA_ROWS)
            qs = pl.ds(pl.multiple_of(i * GRID_W, GRID_W), GRID_W)
            ks = pl.ds(pl.multiple_of(rs * GRID_W, GRID_W), NA_KEYS)
            s = lax.dot_general(q_ref[qs, :], k_ref[ks, :], (((1,), (1,)), ((), ())),
                                preferred_element_type=F32)
            logits.append(s * ATTN_SCALE + bias_ref[r - rs])
            qss.append(qs)
            kss.append(ks)
        probs, dens = [], []
        for s in logits:
            e = jnp.exp(s - jnp.max(s, axis=-1, keepdims=True))
            dens.append(jnp.sum(e, axis=-1, keepdims=True))
            probs.append(e.astype(BF16))
        for qs, ks, p, den in zip(qss, kss, probs, dens):
            o = jnp.dot(p, v_ref[ks, :], preferred_element_type=F32) / den
            o_ref[qs, :] = o.astype(o_ref.dtype)
        return carry

    lax.fori_loop(0, NA_CHUNK_ROWS // NA_GROUP, group, 0)


def _neighbourhood_attention(qkv, bias):
    tq = NA_CHUNK_ROWS * GRID_W
    return pl.pallas_call(
        _na_body,
        grid=(NA_HEADS, ROWS // NA_CHUNK_ROWS),
        in_specs=[
            pl.BlockSpec((tq, HEAD_DIM), lambda h, c: (c, NA_Q_BLK + h)),
            pl.BlockSpec((SEQ, HEAD_DIM), lambda h, c: (0, NA_K_BLK + h)),
            pl.BlockSpec((SEQ, HEAD_DIM), lambda h, c: (0, NA_V_BLK + h)),
            pl.BlockSpec((NA_ROWS, None, GRID_W, NA_KEYS), lambda h, c: (0, h, 0, 0)),
        ],
        out_specs=pl.BlockSpec((tq, HEAD_DIM), lambda h, c: (c, h)),
        out_shape=jax.ShapeDtypeStruct((SEQ, NA_HEADS * HEAD_DIM), BF16),
        compiler_params=_params(32, 2),
        name="na_attn",
    )(qkv, qkv, qkv, bias)


WG_CHUNK_BLOCKS = 8
WG_KEYS = 3 * WG_BLOCK


def _wg_body(hs_ref, q_ref, k_ref, v_ref, o_ref):
    kv = pl.program_id(0)
    chunk = pl.program_id(1)

    def block(j, carry):
        n = chunk * WG_CHUNK_BLOCKS + j
        start = jnp.clip((n - 1) * WG_BLOCK, 0, SEQ - WG_KEYS)
        ks = pl.ds(pl.multiple_of(start, WG_BLOCK), WG_KEYS)
        qs = pl.ds(pl.multiple_of(j * WG_BLOCK, WG_BLOCK), WG_BLOCK)
        kw = k_ref[ks, :]
        vw = v_ref[ks, :]
        q_pos = n * WG_BLOCK + lax.broadcasted_iota(jnp.int32, (WG_BLOCK, WG_KEYS), 0)
        k_pos = start + lax.broadcasted_iota(jnp.int32, (WG_BLOCK, WG_KEYS), 1)
        dist = jnp.abs(k_pos - q_pos)
        valid = dist <= WG_WINDOW
        dist_f = dist.astype(F32)
        heads = [(slice(g * HEAD_DIM, (g + 1) * HEAD_DIM), hs_ref[0, kv * WG_GROUP + g], hs_ref[1, kv * WG_GROUP + g])
                 for g in range(WG_GROUP)]
        logits = []
        for cols, sink, slope in heads:
            s = lax.dot_general(q_ref[qs, cols], kw, (((1,), (1,)), ((), ())), preferred_element_type=F32)
            logits.append(jnp.where(valid, s * ATTN_SCALE - slope * dist_f, MASKED))
        probs, dens = [], []
        for s, (cols, sink, slope) in zip(logits, heads):
            m = jnp.maximum(jnp.max(s, axis=-1, keepdims=True), sink)
            e = jnp.exp(s - m)
            dens.append(jnp.sum(e, axis=-1, keepdims=True) + jnp.exp(sink - m))
            probs.append(e.astype(BF16))
        for p, den, (cols, sink, slope) in zip(probs, dens, heads):
            o = jnp.dot(p, vw, preferred_element_type=F32) / den
            o_ref[qs, cols] = o.astype(o_ref.dtype)
        return carry

    lax.fori_loop(0, WG_CHUNK_BLOCKS, block, 0)


def _windowed_gqa(qkv, head_scalars):
    tq = WG_CHUNK_BLOCKS * WG_BLOCK
    gw = WG_GROUP * HEAD_DIM
    return pl.pallas_call(
        _wg_body,
        grid=(WG_KV_HEADS, SEQ // tq),
        in_specs=[
            pl.BlockSpec(memory_space=pltpu.SMEM),
            pl.BlockSpec((tq, gw), lambda k, c: (c, WG_Q_BLK + k)),
            pl.BlockSpec((SEQ, HEAD_DIM), lambda k, c: (0, WG_K_BLK + k)),
            pl.BlockSpec((SEQ, HEAD_DIM), lambda k, c: (0, WG_V_BLK + k)),
        ],
        out_specs=pl.BlockSpec((tq, gw), lambda k, c: (c, k)),
        out_shape=jax.ShapeDtypeStruct((SEQ, WG_HEADS * HEAD_DIM), BF16),
        compiler_params=_params(32, 2),
        name="wg_attn",
    )(head_scalars, qkv, qkv, qkv)


CONV_TM = 512
HALO = 8


def _conv_body(bg_ref, cg_ref, hc_ref, cgp_ref, hcp_ref, cgn_ref, hcn_ref, w_ref, o_ref):
    i = pl.program_id(0)
    u = cg_ref[...] * hc_ref[...]
    prev_row = cgp_ref[HALO - 1:HALO, :] * hcp_ref[HALO - 1:HALO, :]
    next_row = cgn_ref[0:1, :] * hcn_ref[0:1, :]
    prev_row = jnp.where(i == 0, 0.0, prev_row)
    next_row = jnp.where(i == pl.num_programs(0) - 1, 0.0, next_row)
    row = lax.broadcasted_iota(jnp.int32, u.shape, 0)
    u_prev = jnp.where(row == 0, prev_row, pltpu.roll(u, 1, 0))
    u_next = jnp.where(row == CONV_TM - 1, next_row, pltpu.roll(u, CONV_TM - 1, 0))
    y = bg_ref[...] * (w_ref[0:1, :] * u_prev + w_ref[1:2, :] * u + w_ref[2:3, :] * u_next)
    o_ref[...] = y.astype(o_ref.dtype)


def _short_conv(rest, conv_w):
    nb = CONV_TM // HALO
    last = SEQ // HALO - 1
    main = lambda col: pl.BlockSpec((CONV_TM, SC_WIDTH), lambda i: (i, col))
    prev = lambda col: pl.BlockSpec((HALO, SC_WIDTH), lambda i: (jnp.maximum(i * nb - 1, 0), col))
    nxt = lambda col: pl.BlockSpec((HALO, SC_WIDTH), lambda i: (jnp.minimum((i + 1) * nb, last), col))
    return pl.pallas_call(
        _conv_body,
        grid=(SEQ // CONV_TM,),
        in_specs=[main(0), main(1), main(2), prev(1), prev(2), nxt(1), nxt(2),
                  pl.BlockSpec((3, SC_WIDTH), lambda i: (0, 0))],
        out_specs=pl.BlockSpec((CONV_TM, SC_WIDTH), lambda i: (i, 0)),
        out_shape=jax.ShapeDtypeStruct((SEQ, SC_WIDTH), BF16),
        compiler_params=_params(40, 1),
        name="short_conv",
    )(rest, rest, rest, rest, rest, rest, rest, conv_w)


MERGE_TM = 512
MERGE_TN = 512


def _merge_body(l_ref, ya_ref, yb_ref, yc_ref, g0_ref, g1_ref, g2_ref, b0_ref, b1_ref, b2_ref, w_ref,
                o_ref, wb_ref):
    @pl.when(pl.program_id(1) == 0)
    def _():
        for n in range(N_BRANCH):
            _cast_into(w_ref.at[n], wb_ref.at[n], BRANCH_WIDTH)

    acc = jnp.zeros((MERGE_TM, MERGE_TN), F32)
    for n, (y_ref, g_ref, b_ref) in enumerate(((ya_ref, g0_ref, b0_ref), (yb_ref, g1_ref, b1_ref),
                                               (yc_ref, g2_ref, b2_ref))):
        branch = jnp.dot(y_ref[...], wb_ref[n], preferred_element_type=F32)
        acc = acc + jax.nn.sigmoid(g_ref[...] + b_ref[...]) * branch
    o_ref[...] = acc.astype(o_ref.dtype)


def _merge(l, ya, yb, yc, rest, b_gate, w_branch):
    y_spec = pl.BlockSpec((MERGE_TM, BRANCH_WIDTH), lambda j, i, l_ref: (i, 0))
    gcol = lambda n: (GATE_COL + n * D_MODEL) // MERGE_TN
    g_spec = lambda n: pl.BlockSpec((MERGE_TM, MERGE_TN), lambda j, i, l_ref: (i, gcol(n) + j))
    b_spec = lambda n: pl.BlockSpec((1, MERGE_TN), lambda j, i, l_ref: (0, n * D_MODEL // MERGE_TN + j))
    return pl.pallas_call(
        _merge_body,
        grid_spec=pltpu.PrefetchScalarGridSpec(
            num_scalar_prefetch=1,
            grid=(D_MODEL // MERGE_TN, SEQ // MERGE_TM),
            in_specs=[y_spec, y_spec, y_spec, g_spec(0), g_spec(1), g_spec(2), b_spec(0), b_spec(1), b_spec(2),
                      pl.BlockSpec((None, N_BRANCH, BRANCH_WIDTH, MERGE_TN),
                                   lambda j, i, l_ref: (l_ref[0], 0, 0, j))],
            out_specs=pl.BlockSpec((MERGE_TM, MERGE_TN), lambda j, i, l_ref: (i, j)),
            scratch_shapes=[pltpu.VMEM((N_BRANCH, BRANCH_WIDTH, MERGE_TN), BF16)],
        ),
        out_shape=jax.ShapeDtypeStruct((SEQ, D_MODEL), BF16),
        compiler_params=_params(48, 2),
        name="branch_merge",
    )(l, ya, yb, yc, rest, rest, rest, b_gate, b_gate, b_gate, w_branch)


def _layer_norm(y, g, b):
    mu = jnp.mean(y, axis=-1, keepdims=True)
    yc = y - mu
    var = jnp.mean(yc * yc, axis=-1, keepdims=True)
    return yc * lax.rsqrt(var + LN_EPS) * g + b


OUT_TM = 256


def _out_body(m_ref, w_ref, x_ref, g_ref, b_ref, wr_ref, xo_ref, res_ref, aff_ref):
    mix = jnp.dot(m_ref[...], w_ref[...], preferred_element_type=F32)
    x1 = _layer_norm(ALPHA * x_ref[...] + mix, g_ref[...], b_ref[...])
    xo_ref[...] = x1
    res_ref[...] = ALPHA * x1
    logits = lax.dot_general(wr_ref[...], x1.astype(BF16), (((1,), (1,)), ((), ())),
                             preferred_element_type=F32)
    e = jnp.exp(logits - jnp.max(logits, axis=0, keepdims=True))
    aff_ref[...] = e / jnp.sum(e, axis=0, keepdims=True)


def _out_proj_ln(merged, w_out_b, x, g, b, w_router_t):
    row = pl.BlockSpec((OUT_TM, D_MODEL), lambda i: (i, 0))
    full = lambda a: pl.BlockSpec(a.shape, lambda i: (0,) * a.ndim)
    return pl.pallas_call(
        _out_body,
        grid=(SEQ // OUT_TM,),
        in_specs=[row, full(w_out_b), row, full(g), full(b), full(w_router_t)],
        out_specs=[row, row, pl.BlockSpec((N_EXPERTS, OUT_TM), lambda i: (0, i))],
        out_shape=[jax.ShapeDtypeStruct((SEQ, D_MODEL), F32), jax.ShapeDtypeStruct((SEQ, D_MODEL), F32),
                   jax.ShapeDtypeStruct((N_EXPERTS, SEQ), F32)],
        compiler_params=_params(48, 1),
        name="out_proj_ln",
    )(merged, w_out_b, x, g, b, w_router_t)


MIX_TM = 256
GATE_TN = 1024


def _mix_body(ya_ref, yb_ref, yc_ref, g00, g01, g10, g11, g20, g21, bg_ref, wb_ref, wo_ref, x_ref, g_ref, b_ref,
              wr_ref, xo_ref, res_ref, aff_ref):
    y_refs = (ya_ref, yb_ref, yc_ref)
    gate_refs = ((g00, g01), (g10, g11), (g20, g21))
    halves = []
    for half in range(D_MODEL // GATE_TN):
        acc = jnp.zeros((MIX_TM, GATE_TN), F32)
        for n in range(N_BRANCH):
            col0 = n * D_MODEL + half * GATE_TN
            branch = jnp.dot(y_refs[n][...],
                             wb_ref[n * BRANCH_WIDTH:(n + 1) * BRANCH_WIDTH, half * GATE_TN:(half + 1) * GATE_TN],
                             preferred_element_type=F32)
            gate = jax.nn.sigmoid(gate_refs[n][half][...] + bg_ref[:, col0:col0 + GATE_TN])
            acc = acc + gate * branch
        halves.append(acc.astype(BF16))
    mix = sum(jnp.dot(m, wo_ref[h * GATE_TN:(h + 1) * GATE_TN, :], preferred_element_type=F32)
              for h, m in enumerate(halves))
    x1 = _layer_norm(ALPHA * x_ref[...] + mix, g_ref[...], b_ref[...])
    xo_ref[...] = x1
    res_ref[...] = ALPHA * x1
    logits = lax.dot_general(wr_ref[...], x1.astype(BF16), (((1,), (1,)), ((), ())),
                             preferred_element_type=F32)
    e = jnp.exp(logits - jnp.max(logits, axis=0, keepdims=True))
    aff_ref[...] = e / jnp.sum(e, axis=0, keepdims=True)


def _mix_ln(ya, yb, yc, rest, b_gate, w_branch_b, w_out_b, x, g, b, w_router_t):
    row = lambda n: pl.BlockSpec((MIX_TM, n), lambda i: (i, 0))
    gate = lambda n, half: pl.BlockSpec(
        (MIX_TM, GATE_TN), lambda i: (i, (GATE_COL + n * D_MODEL) // GATE_TN + half))
    full = lambda a: pl.BlockSpec(a.shape, lambda i: (0,) * a.ndim)
    once = lambda a: pl.BlockSpec(a.shape, lambda i: (0,) * a.ndim, pipeline_mode=pl.Buffered(1))
    return pl.pallas_call(
        _mix_body,
        grid=(SEQ // MIX_TM,),
        in_specs=[row(BRANCH_WIDTH), row(BRANCH_WIDTH), row(BRANCH_WIDTH)]
                 + [gate(n, half) for n in range(N_BRANCH) for half in range(2)]
                 + [full(b_gate), once(w_branch_b), once(w_out_b), row(D_MODEL), full(g), full(b), full(w_router_t)],
        out_specs=[row(D_MODEL), row(D_MODEL), pl.BlockSpec((N_EXPERTS, MIX_TM), lambda i: (0, i))],
        out_shape=[jax.ShapeDtypeStruct((SEQ, D_MODEL), F32), jax.ShapeDtypeStruct((SEQ, D_MODEL), F32),
                   jax.ShapeDtypeStruct((N_EXPERTS, SEQ), F32)],
        compiler_params=_params(58, 1),
        name="mix_ln",
    )(ya, yb, yc, rest, rest, rest, rest, rest, rest, b_gate, w_branch_b, w_out_b, x, g, b, w_router_t)


FINAL_TM = 512


def _final_body(y_ref, g_ref, b_ref, xo_ref, xb_ref):
    x2 = _layer_norm(y_ref[...], g_ref[...], b_ref[...])
    xo_ref[...] = x2
    xb_ref[...] = x2.astype(BF16)


def _final_ln(y, g, b):
    row = pl.BlockSpec((FINAL_TM, D_MODEL), lambda i: (i, 0))
    vec = pl.BlockSpec((1, D_MODEL), lambda i: (0, 0))
    return pl.pallas_call(
        _final_body,
        grid=(SEQ // FINAL_TM,),
        in_specs=[row, vec, vec],
        out_specs=[row, row],
        out_shape=[jax.ShapeDtypeStruct((SEQ, D_MODEL), F32), jax.ShapeDtypeStruct((SEQ, D_MODEL), BF16)],
        compiler_params=_params(48, 1),
        name="final_ln",
    )(y, g, b)


FFN_TF = 256
FFN_STEPS = D_FF // FFN_TF
GATHER_STEPS = 4
GATHER_ROWS = CAPACITY // GATHER_STEPS
XROWS_PER_STEP = 168
XROWS_TAIL = CAPACITY - FFN_STEPS * XROWS_PER_STEP
DOWN_HALF = D_MODEL // 2


def _row_copy(x_hbm, rows_ref, sem, src_row, dst_row):
    return pltpu.make_async_copy(x_hbm.at[pl.ds(src_row, 1), :], rows_ref.at[pl.ds(dst_row, 1), :], sem)


ROW_GROUP = 8


def _start_rows(x_hbm, rows_ref, sem, idx_ref, first, count):
    def issue(g, c):
        base = pl.multiple_of(first + g * ROW_GROUP, ROW_GROUP)
        for t in range(ROW_GROUP):
            _row_copy(x_hbm, rows_ref, sem, idx_ref[0, base + t], base + t).start()
        return c
    lax.fori_loop(0, count // ROW_GROUP, issue, 0)


def _wait_rows(x_hbm, rows_ref, sem, count):
    def wait(j, c):
        _row_copy(x_hbm, rows_ref, sem, 0, 0).wait()
        return c
    lax.fori_loop(0, count, wait, 0, unroll=8)


def _ffn_body(l_ref, idx_ref, idx_next_ref, x_hbm, res_hbm, gate_ref, wg_ref, wu_ref, wd_ref, acc_hbm,
              rows_ref, xe_ref, ffn_ref, accb_ref, sem_x, sem_g, sem_s):
    del res_hbm
    e = pl.program_id(0)
    f = pl.program_id(1)
    last = f == FFN_STEPS - 1

    @pl.when((e == 0) & (f == 0))
    def _():
        _start_rows(x_hbm, rows_ref, sem_x, idx_ref, 0, CAPACITY)

    @pl.when(f == 0)
    def _():
        _wait_rows(x_hbm, rows_ref, sem_x, CAPACITY)
        _cast_into(rows_ref, xe_ref, CAPACITY)
        _start_rows(x_hbm, rows_ref, sem_x, idx_next_ref, FFN_STEPS * XROWS_PER_STEP, XROWS_TAIL)

        def zero(i, c):
            sl = pl.ds(pl.multiple_of(i * CAST_ROWS, CAST_ROWS), CAST_ROWS)
            ffn_ref[sl, :] = jnp.zeros((CAST_ROWS, D_MODEL), F32)
            return c
        lax.fori_loop(0, CAPACITY // CAST_ROWS, zero, 0)

    @pl.when((e > 0) & (f == 1))
    def _():
        _wait_rows(accb_ref, acc_hbm, sem_s, CAPACITY)

    @pl.when((f >= 1) & (f <= GATHER_STEPS))
    def _():
        _start_rows(acc_hbm, accb_ref, sem_g, idx_ref, (f - 1) * GATHER_ROWS, GATHER_ROWS)

    xe = xe_ref[...]
    a = jnp.dot(xe, wg_ref[...].astype(BF16), preferred_element_type=F32)
    for t in range(XROWS_PER_STEP):
        j = f * XROWS_PER_STEP + t
        _row_copy(x_hbm, rows_ref, sem_x, idx_next_ref[0, j], j).start()
    u = jnp.dot(xe, wu_ref[...].astype(BF16), preferred_element_type=F32)
    hid = (jax.nn.silu(a) * u).astype(BF16)
    for half in range(2):
        cols = slice(half * DOWN_HALF, (half + 1) * DOWN_HALF)
        ffn_ref[:, cols] += jnp.dot(hid, wd_ref[:, cols].astype(BF16), preferred_element_type=F32)

    @pl.when(last)
    def _():
        _wait_rows(acc_hbm, accb_ref, sem_g, CAPACITY)

        def update(i, c):
            sl = pl.ds(pl.multiple_of(i * CAST_ROWS, CAST_ROWS), CAST_ROWS)
            accb_ref[sl, :] += ffn_ref[sl, :] * gate_ref[sl, :]
            return c
        lax.fori_loop(0, CAPACITY // CAST_ROWS, update, 0)

        def issue(g, c):
            base = pl.multiple_of(g * ROW_GROUP, ROW_GROUP)
            for t in range(ROW_GROUP):
                _row_copy(accb_ref, acc_hbm, sem_s, base + t, idx_ref[0, base + t]).start()
            return c
        lax.fori_loop(0, CAPACITY // ROW_GROUP, issue, 0)

    @pl.when(last & (e == N_EXPERTS - 1))
    def _():
        _wait_rows(accb_ref, acc_hbm, sem_s, CAPACITY)
        _wait_rows(x_hbm, rows_ref, sem_x, CAPACITY)


def _expert_ffn_accumulate(l, idx, x1, res, gates, w_gate, w_up, w_down):
    cur = lambda e, f, l_ref: (e, 0, 0)
    nxt = lambda e, f, l_ref: (jnp.minimum(e + 1, N_EXPERTS - 1), 0, 0)
    rows = lambda dt: pltpu.VMEM((CAPACITY, D_MODEL), dt)
    return pl.pallas_call(
        _ffn_body,
        grid_spec=pltpu.PrefetchScalarGridSpec(
            num_scalar_prefetch=1,
            grid=(N_EXPERTS, FFN_STEPS),
            in_specs=[
                pl.BlockSpec((None, 1, CAPACITY), cur, memory_space=pltpu.SMEM),
                pl.BlockSpec((None, 1, CAPACITY), nxt, memory_space=pltpu.SMEM),
                pl.BlockSpec(memory_space=pl.ANY),
                pl.BlockSpec(memory_space=pl.ANY),
                pl.BlockSpec((None, CAPACITY, 1), cur),
                pl.BlockSpec((None, None, D_MODEL, FFN_TF), lambda e, f, l_ref: (l_ref[0], e, 0, f)),
                pl.BlockSpec((None, None, D_MODEL, FFN_TF), lambda e, f, l_ref: (l_ref[0], e, 0, f)),
                pl.BlockSpec((None, None, FFN_TF, D_MODEL), lambda e, f, l_ref: (l_ref[0], e, f, 0)),
            ],
            out_specs=pl.BlockSpec(memory_space=pl.ANY),
            scratch_shapes=[rows(F32), rows(BF16), rows(F32), rows(F32),
                            pltpu.SemaphoreType.DMA, pltpu.SemaphoreType.DMA, pltpu.SemaphoreType.DMA],
        ),
        out_shape=jax.ShapeDtypeStruct((SEQ, D_MODEL), F32),
        input_output_aliases={4: 0},
        compiler_params=_params(58, 2),
        name="expert_ffn",
    )(l, idx, idx, x1, res, gates, w_gate, w_up, w_down)


def kernel(x, w_in, b_gate, rpb, sink, conv_w, w_branch, w_out, ln_g, ln_b, w_router, w_gate, w_up, w_down):
    x0 = x.reshape(SEQ, D_MODEL)
    na_bias = _na_bias_tables(rpb)
    slopes = 2.0 ** (-8.0 * jnp.arange(1, WG_HEADS + 1, dtype=F32) / WG_HEADS)
    head_scalars = jnp.stack([sink, jnp.broadcast_to(slopes, sink.shape)], axis=1)
    w_router_t = jnp.swapaxes(w_router, 1, 2).astype(BF16)

    def layer(carry, per_layer):
        xf, xb = carry
        li, bias_l, hs_l, conv_l, bg_l, g_l, b_l, wr_l = per_layer
        l = li.reshape(1)
        qkv = _project(l, xb, w_in, 0, QKV_WIDTH, BF16)
        rest = _project(l, xb, w_in, QKV_WIDTH, REST_WIDTH, F32)
        ya = _neighbourhood_attention(qkv, bias_l)
        yb = _windowed_gqa(qkv, hs_l)
        yc = _short_conv(rest, conv_l)
        w_branch_b = _cast_layer(l, w_branch.reshape(DEPTH, N_BRANCH * BRANCH_WIDTH, D_MODEL), 512)
        w_out_b = _cast_layer(l, w_out, 512)
        x1, res, aff_t = _mix_ln(ya, yb, yc, rest, bg_l.reshape(1, N_BRANCH * D_MODEL), w_branch_b, w_out_b,
                                 xf, g_l[0:1], b_l[0:1], wr_l)
        gates, idx = lax.top_k(aff_t, CAPACITY)
        y = _expert_ffn_accumulate(l, idx[:, None, :], x1, res, gates[..., None], w_gate, w_up, w_down)
        x2, x2b = _final_ln(y, g_l[1:2], b_l[1:2])
        return (x2, x2b), None

    per_layer = (jnp.arange(DEPTH, dtype=jnp.int32), na_bias, head_scalars, conv_w, b_gate, ln_g, ln_b, w_router_t)
    (xf, _), _ = lax.scan(layer, (x0, _cast_rows(x0, 512)), per_layer)
    return xf.reshape(x.shape)
```

```python
import functools

import jax
import jax.numpy as jnp
from jax import lax
from jax.experimental import pallas as pl
from jax.experimental.pallas import tpu as pltpu

F32 = jnp.float32
BF16 = jnp.bfloat16

D_MODEL = 2048
SEQ = 8192
DEPTH = 4
GRID_W = 64
ROWS = SEQ // GRID_W
HEAD_DIM = 128
NA_HEADS = 8
NA_ROWS = 8
NA_COLS = 16
WG_HEADS = 8
WG_KV_HEADS = 2
WG_GROUP = WG_HEADS // WG_KV_HEADS
WG_WINDOW = 128
WG_BLOCK = 128
SC_WIDTH = 1024
BRANCH_WIDTH = 1024
N_BRANCH = 3
N_EXPERTS = 16
CAPACITY = 2 * SEQ // N_EXPERTS
D_FF = 1536
ALPHA = (2 * DEPTH) ** 0.25
LN_EPS = 1e-5
ATTN_SCALE = HEAD_DIM ** -0.5
MASKED = -1e30

QKV_WIDTH = 3 * NA_HEADS * HEAD_DIM + (WG_HEADS + 2 * WG_KV_HEADS) * HEAD_DIM
REST_WIDTH = 3 * SC_WIDTH + N_BRANCH * D_MODEL
NA_Q_BLK, NA_K_BLK, NA_V_BLK = 0, NA_HEADS, 2 * NA_HEADS
WG_Q_BLK = 3 * NA_HEADS * HEAD_DIM // (WG_GROUP * HEAD_DIM)
WG_K_BLK = (3 * NA_HEADS + WG_HEADS) * HEAD_DIM // HEAD_DIM
WG_V_BLK = WG_K_BLK + WG_KV_HEADS
GATE_COL = 3 * SC_WIDTH

MIB = 1024 * 1024


def _params(vmem_mib, n_axes):
    return pltpu.CompilerParams(dimension_semantics=("arbitrary",) * n_axes,
                                vmem_limit_bytes=vmem_mib * MIB)


def _cast_body(x_ref, o_ref):
    o_ref[...] = x_ref[...].astype(BF16)


def _cast_rows(x, tm):
    m, n = x.shape
    return pl.pallas_call(
        _cast_body,
        grid=(m // tm,),
        in_specs=[pl.BlockSpec((tm, n), lambda i: (i, 0))],
        out_specs=pl.BlockSpec((tm, n), lambda i: (i, 0)),
        out_shape=jax.ShapeDtypeStruct((m, n), BF16),
        compiler_params=_params(32, 1),
        name="cast_rows",
    )(x)


def _cast_layer_body(l_ref, x_ref, o_ref):
    o_ref[...] = x_ref[...].astype(BF16)


def _cast_layer(l, w, tm):
    _, m, n = w.shape
    return pl.pallas_call(
        _cast_layer_body,
        grid_spec=pltpu.PrefetchScalarGridSpec(
            num_scalar_prefetch=1,
            grid=(m // tm,),
            in_specs=[pl.BlockSpec((None, tm, n), lambda i, l_ref: (l_ref[0], i, 0))],
            out_specs=pl.BlockSpec((tm, n), lambda i, l_ref: (i, 0)),
        ),
        out_shape=jax.ShapeDtypeStruct((m, n), BF16),
        compiler_params=_params(32, 1),
        name="cast_layer",
    )(l, w)


PROJ_TM = 1024
PROJ_TN = 1536
CAST_ROWS = 256


def _cast_into(w_ref, wb_ref, rows):
    def chunk(i, c):
        sl = pl.ds(pl.multiple_of(i * CAST_ROWS, CAST_ROWS), CAST_ROWS)
        wb_ref[sl, :] = w_ref[sl, :].astype(BF16)
        return c
    lax.fori_loop(0, rows // CAST_ROWS, chunk, 0)


def _proj_body(l_ref, x_ref, w_ref, o_ref, wb_ref):
    @pl.when(pl.program_id(1) == 0)
    def _():
        _cast_into(w_ref, wb_ref, D_MODEL)

    o_ref[...] = jnp.dot(x_ref[...], wb_ref[...], preferred_element_type=F32).astype(o_ref.dtype)


def _project(l, xb, w_in, col0, width, out_dtype):
    n_tiles, off = width // PROJ_TN, col0 // PROJ_TN
    return pl.pallas_call(
        _proj_body,
        grid_spec=pltpu.PrefetchScalarGridSpec(
            num_scalar_prefetch=1,
            grid=(n_tiles, SEQ // PROJ_TM),
            in_specs=[
                pl.BlockSpec((PROJ_TM, D_MODEL), lambda j, i, l_ref: (i, 0)),
                pl.BlockSpec((None, D_MODEL, PROJ_TN), lambda j, i, l_ref: (l_ref[0], 0, off + j)),
            ],
            out_specs=pl.BlockSpec((PROJ_TM, PROJ_TN), lambda j, i, l_ref: (i, j)),
            scratch_shapes=[pltpu.VMEM((D_MODEL, PROJ_TN), BF16)],
        ),
        out_shape=jax.ShapeDtypeStruct((SEQ, width), out_dtype),
        compiler_params=_params(58, 2),
        name="in_proj",
    )(l, xb, w_in)


NA_CHUNK_ROWS = 32
NA_GROUP = 16
NA_KEYS = NA_ROWS * GRID_W


def _na_bias_tables(rpb):
    c = jnp.arange(GRID_W)[:, None]
    cj = jnp.arange(GRID_W)[None, :]
    cs = jnp.clip(c - NA_COLS // 2, 0, GRID_W - NA_COLS)
    valid = (cj >= cs) & (cj < cs + NA_COLS)
    select = ((cj - c + (NA_COLS - 1))[None] == jnp.arange(2 * NA_COLS - 1)[:, None, None]) & valid[None]
    t = jnp.einsum('lhrk,kcj->lhcrj', rpb.astype(F32), select.astype(F32),
                   precision=lax.Precision.HIGHEST)
    t = jnp.where(valid[None, None, :, None, :], t, MASKED)
    per_d = [t[:, :, :, NA_ROWS - 1 - d:2 * NA_ROWS - 1 - d, :].reshape(DEPTH, NA_HEADS, GRID_W, NA_KEYS)
             for d in range(NA_ROWS)]
    return jnp.stack(per_d, axis=1)


def _na_body(q_ref, k_ref, v_ref, bias_ref, o_ref):
    chunk = pl.program_id(1)

    def group(gi, carry):
        qss, kss, logits = [], [], []
        for t in range(NA_GROUP):
            i = gi * NA_GROUP + t
            r = chunk * NA_CHUNK_ROWS + i
            rs = jnp.clip(r - NA_ROWS // 2, 0, ROWS - NA_ROWS)
            qs = pl.ds(pl.multiple_of(i * GRID_W, GRID_W), GRID_W)
            ks = pl.ds(pl.multiple_of(rs * GRID_W, GRID_W), NA_KEYS)
            s = lax.dot_general(q_ref[qs, :], k_ref[ks, :], (((1,), (1,)), ((), ())),
                                preferred_element_type=F32)
            logits.append(s * ATTN_SCALE + bias_ref[r - rs])
            qss.append(qs)
            kss.append(ks)
        probs, dens = [], []
        for s in logits:
            e = jnp.exp(s - jnp.max(s, axis=-1, keepdims=True))
            dens.append(jnp.sum(e, axis=-1, keepdims=True))
            probs.append(e.astype(BF16))
        for qs, ks, p, den in zip(qss, kss, probs, dens):
            o = jnp.dot(p, v_ref[ks, :], preferred_element_type=F32) / den
            o_ref[qs, :] = o.astype(o_ref.dtype)
        return carry

    lax.fori_loop(0, NA_CHUNK_ROWS // NA_GROUP, group, 0)


def _neighbourhood_attention(qkv, bias):
    tq = NA_CHUNK_ROWS * GRID_W
    return pl.pallas_call(
        _na_body,
        grid=(NA_HEADS, ROWS // NA_CHUNK_ROWS),
        in_specs=[
            pl.BlockSpec((tq, HEAD_DIM), lambda h, c: (c, NA_Q_BLK + h)),
            pl.BlockSpec((SEQ, HEAD_DIM), lambda h, c: (0, NA_K_BLK + h)),
            pl.BlockSpec((SEQ, HEAD_DIM), lambda h, c: (0, NA_V_BLK + h)),
            pl.BlockSpec((NA_ROWS, None, GRID_W, NA_KEYS), lambda h, c: (0, h, 0, 0)),
        ],
        out_specs=pl.BlockSpec((tq, HEAD_DIM), lambda h, c: (c, h)),
        out_shape=jax.ShapeDtypeStruct((SEQ, NA_HEADS * HEAD_DIM), BF16),
        compiler_params=_params(32, 2),
        name="na_attn",
    )(qkv, qkv, qkv, bias)


WG_CHUNK_BLOCKS = 8
WG_PAIR = 2
WG_KEYS = 3 * WG_BLOCK


def _wg_body(hs_ref, q_ref, k_ref, v_ref, o_ref):
    kv = pl.program_id(0)
    chunk = pl.program_id(1)

    heads = [(slice(g * HEAD_DIM, (g + 1) * HEAD_DIM), hs_ref[0, kv * WG_GROUP + g], hs_ref[1, kv * WG_GROUP + g])
             for g in range(WG_GROUP)]

    def blocks(jj, carry):
        work, logits = [], []
        for t in range(WG_PAIR):
            j = jj * WG_PAIR + t
            n = chunk * WG_CHUNK_BLOCKS + j
            start = jnp.clip((n - 1) * WG_BLOCK, 0, SEQ - WG_KEYS)
            ks = pl.ds(pl.multiple_of(start, WG_BLOCK), WG_KEYS)
            qs = pl.ds(pl.multiple_of(j * WG_BLOCK, WG_BLOCK), WG_BLOCK)
            kw = k_ref[ks, :]
            q_pos = n * WG_BLOCK + lax.broadcasted_iota(jnp.int32, (WG_BLOCK, WG_KEYS), 0)
            k_pos = start + lax.broadcasted_iota(jnp.int32, (WG_BLOCK, WG_KEYS), 1)
            dist = jnp.abs(k_pos - q_pos)
            valid = dist <= WG_WINDOW
            dist_f = dist.astype(F32)
            for cols, sink, slope in heads:
                s = lax.dot_general(q_ref[qs, cols], kw, (((1,), (1,)), ((), ())), preferred_element_type=F32)
                logits.append(jnp.where(valid, s * ATTN_SCALE - slope * dist_f, MASKED))
                work.append((qs, ks, cols, sink))
        probs, dens = [], []
        for s, (qs, ks, cols, sink) in zip(logits, work):
            m = jnp.maximum(jnp.max(s, axis=-1, keepdims=True), sink)
            e = jnp.exp(s - m)
            dens.append(jnp.sum(e, axis=-1, keepdims=True) + jnp.exp(sink - m))
            probs.append(e.astype(BF16))
        for p, den, (qs, ks, cols, sink) in zip(probs, dens, work):
            o = jnp.dot(p, v_ref[ks, :], preferred_element_type=F32) / den
            o_ref[qs, cols] = o.astype(o_ref.dtype)
        return carry

    lax.fori_loop(0, WG_CHUNK_BLOCKS // WG_PAIR, blocks, 0)


def _windowed_gqa(qkv, head_scalars):
    tq = WG_CHUNK_BLOCKS * WG_BLOCK
    gw = WG_GROUP * HEAD_DIM
    return pl.pallas_call(
        _wg_body,
        grid=(WG_KV_HEADS, SEQ // tq),
        in_specs=[
            pl.BlockSpec(memory_space=pltpu.SMEM),
            pl.BlockSpec((tq, gw), lambda k, c: (c, WG_Q_BLK + k)),
            pl.BlockSpec((SEQ, HEAD_DIM), lambda k, c: (0, WG_K_BLK + k)),
            pl.BlockSpec((SEQ, HEAD_DIM), lambda k, c: (0, WG_V_BLK + k)),
        ],
        out_specs=pl.BlockSpec((tq, gw), lambda k, c: (c, k)),
        out_shape=jax.ShapeDtypeStruct((SEQ, WG_HEADS * HEAD_DIM), BF16),
        compiler_params=_params(32, 2),
        name="wg_attn",
    )(head_scalars, qkv, qkv, qkv)


CONV_TM = 512
HALO = 8


def _conv_body(bg_ref, cg_ref, hc_ref, cgp_ref, hcp_ref, cgn_ref, hcn_ref, w_ref, o_ref):
    i = pl.program_id(0)
    u = cg_ref[...] * hc_ref[...]
    prev_row = cgp_ref[HALO - 1:HALO, :] * hcp_ref[HALO - 1:HALO, :]
    next_row = cgn_ref[0:1, :] * hcn_ref[0:1, :]
    prev_row = jnp.where(i == 0, 0.0, prev_row)
    next_row = jnp.where(i == pl.num_programs(0) - 1, 0.0, next_row)
    row = lax.broadcasted_iota(jnp.int32, u.shape, 0)
    u_prev = jnp.where(row == 0, prev_row, pltpu.roll(u, 1, 0))
    u_next = jnp.where(row == CONV_TM - 1, next_row, pltpu.roll(u, CONV_TM - 1, 0))
    y = bg_ref[...] * (w_ref[0:1, :] * u_prev + w_ref[1:2, :] * u + w_ref[2:3, :] * u_next)
    o_ref[...] = y.astype(o_ref.dtype)


def _short_conv(rest, conv_w):
    nb = CONV_TM // HALO
    last = SEQ // HALO - 1
    main = lambda col: pl.BlockSpec((CONV_TM, SC_WIDTH), lambda i: (i, col))
    prev = lambda col: pl.BlockSpec((HALO, SC_WIDTH), lambda i: (jnp.maximum(i * nb - 1, 0), col))
    nxt = lambda col: pl.BlockSpec((HALO, SC_WIDTH), lambda i: (jnp.minimum((i + 1) * nb, last), col))
    return pl.pallas_call(
        _conv_body,
        grid=(SEQ // CONV_TM,),
        in_specs=[main(0), main(1), main(2), prev(1), prev(2), nxt(1), nxt(2),
                  pl.BlockSpec((3, SC_WIDTH), lambda i: (0, 0))],
        out_specs=pl.BlockSpec((CONV_TM, SC_WIDTH), lambda i: (i, 0)),
        out_shape=jax.ShapeDtypeStruct((SEQ, SC_WIDTH), BF16),
        compiler_params=_params(40, 1),
        name="short_conv",
    )(rest, rest, rest, rest, rest, rest, rest, conv_w)


MERGE_TM = 512
MERGE_TN = 512


def _merge_body(l_ref, ya_ref, yb_ref, yc_ref, g0_ref, g1_ref, g2_ref, b0_ref, b1_ref, b2_ref, w_ref,
                o_ref, wb_ref):
    @pl.when(pl.program_id(1) == 0)
    def _():
        for n in range(N_BRANCH):
            _cast_into(w_ref.at[n], wb_ref.at[n], BRANCH_WIDTH)

    acc = jnp.zeros((MERGE_TM, MERGE_TN), F32)
    for n, (y_ref, g_ref, b_ref) in enumerate(((ya_ref, g0_ref, b0_ref), (yb_ref, g1_ref, b1_ref),
                                               (yc_ref, g2_ref, b2_ref))):
        branch = jnp.dot(y_ref[...], wb_ref[n], preferred_element_type=F32)
        acc = acc + jax.nn.sigmoid(g_ref[...] + b_ref[...]) * branch
    o_ref[...] = acc.astype(o_ref.dtype)


def _merge(l, ya, yb, yc, rest, b_gate, w_branch):
    y_spec = pl.BlockSpec((MERGE_TM, BRANCH_WIDTH), lambda j, i, l_ref: (i, 0))
    gcol = lambda n: (GATE_COL + n * D_MODEL) // MERGE_TN
    g_spec = lambda n: pl.BlockSpec((MERGE_TM, MERGE_TN), lambda j, i, l_ref: (i, gcol(n) + j))
    b_spec = lambda n: pl.BlockSpec((1, MERGE_TN), lambda j, i, l_ref: (0, n * D_MODEL // MERGE_TN + j))
    return pl.pallas_call(
        _merge_body,
        grid_spec=pltpu.PrefetchScalarGridSpec(
            num_scalar_prefetch=1,
            grid=(D_MODEL // MERGE_TN, SEQ // MERGE_TM),
            in_specs=[y_spec, y_spec, y_spec, g_spec(0), g_spec(1), g_spec(2), b_spec(0), b_spec(1), b_spec(2),
                      pl.BlockSpec((None, N_BRANCH, BRANCH_WIDTH, MERGE_TN),
                                   lambda j, i, l_ref: (l_ref[0], 0, 0, j))],
            out_specs=pl.BlockSpec((MERGE_TM, MERGE_TN), lambda j, i, l_ref: (i, j)),
            scratch_shapes=[pltpu.VMEM((N_BRANCH, BRANCH_WIDTH, MERGE_TN), BF16)],
        ),
        out_shape=jax.ShapeDtypeStruct((SEQ, D_MODEL), BF16),
        compiler_params=_params(48, 2),
        name="branch_merge",
    )(l, ya, yb, yc, rest, rest, rest, b_gate, b_gate, b_gate, w_branch)


def _layer_norm(y, g, b):
    mu = jnp.mean(y, axis=-1, keepdims=True)
    yc = y - mu
    var = jnp.mean(yc * yc, axis=-1, keepdims=True)
    return yc * lax.rsqrt(var + LN_EPS) * g + b


OUT_TM = 256


def _out_body(m_ref, w_ref, x_ref, g_ref, b_ref, wr_ref, xo_ref, res_ref, aff_ref):
    mix = jnp.dot(m_ref[...], w_ref[...], preferred_element_type=F32)
    x1 = _layer_norm(ALPHA * x_ref[...] + mix, g_ref[...], b_ref[...])
    xo_ref[...] = x1
    res_ref[...] = ALPHA * x1
    logits = lax.dot_general(wr_ref[...], x1.astype(BF16), (((1,), (1,)), ((), ())),
                             preferred_element_type=F32)
    e = jnp.exp(logits - jnp.max(logits, axis=0, keepdims=True))
    aff_ref[...] = e / jnp.sum(e, axis=0, keepdims=True)


def _out_proj_ln(merged, w_out_b, x, g, b, w_router_t):
    row = pl.BlockSpec((OUT_TM, D_MODEL), lambda i: (i, 0))
    full = lambda a: pl.BlockSpec(a.shape, lambda i: (0,) * a.ndim)
    return pl.pallas_call(
        _out_body,
        grid=(SEQ // OUT_TM,),
        in_specs=[row, full(w_out_b), row, full(g), full(b), full(w_router_t)],
        out_specs=[row, row, pl.BlockSpec((N_EXPERTS, OUT_TM), lambda i: (0, i))],
        out_shape=[jax.ShapeDtypeStruct((SEQ, D_MODEL), F32), jax.ShapeDtypeStruct((SEQ, D_MODEL), F32),
                   jax.ShapeDtypeStruct((N_EXPERTS, SEQ), F32)],
        compiler_params=_params(48, 1),
        name="out_proj_ln",
    )(merged, w_out_b, x, g, b, w_router_t)


MIX_TM = 256
GATE_TN = 1024


def _mix_body(ya_ref, yb_ref, yc_ref, g00, g01, g10, g11, g20, g21, bg_ref, wb_ref, wo_ref, x_ref, g_ref, b_ref,
              wr_ref, xo_ref, res_ref, aff_ref):
    y_refs = (ya_ref, yb_ref, yc_ref)
    gate_refs = ((g00, g01), (g10, g11), (g20, g21))
    halves = []
    for half in range(D_MODEL // GATE_TN):
        acc = jnp.zeros((MIX_TM, GATE_TN), F32)
        for n in range(N_BRANCH):
            col0 = n * D_MODEL + half * GATE_TN
            branch = jnp.dot(y_refs[n][...],
                             wb_ref[n * BRANCH_WIDTH:(n + 1) * BRANCH_WIDTH, half * GATE_TN:(half + 1) * GATE_TN],
                             preferred_element_type=F32)
            gate = jax.nn.sigmoid(gate_refs[n][half][...] + bg_ref[:, col0:col0 + GATE_TN])
            acc = acc + gate * branch
        halves.append(acc.astype(BF16))
    mix = sum(jnp.dot(m, wo_ref[h * GATE_TN:(h + 1) * GATE_TN, :], preferred_element_type=F32)
              for h, m in enumerate(halves))
    x1 = _layer_norm(ALPHA * x_ref[...] + mix, g_ref[...], b_ref[...])
    xo_ref[...] = x1
    res_ref[...] = ALPHA * x1
    logits = lax.dot_general(wr_ref[...], x1.astype(BF16), (((1,), (1,)), ((), ())),
                             preferred_element_type=F32)
    e = jnp.exp(logits - jnp.max(logits, axis=0, keepdims=True))
    aff_ref[...] = e / jnp.sum(e, axis=0, keepdims=True)


def _mix_ln(ya, yb, yc, rest, b_gate, w_branch_b, w_out_b, x, g, b, w_router_t):
    row = lambda n: pl.BlockSpec((MIX_TM, n), lambda i: (i, 0))
    gate = lambda n, half: pl.BlockSpec(
        (MIX_TM, GATE_TN), lambda i: (i, (GATE_COL + n * D_MODEL) // GATE_TN + half))
    full = lambda a: pl.BlockSpec(a.shape, lambda i: (0,) * a.ndim)
    once = lambda a: pl.BlockSpec(a.shape, lambda i: (0,) * a.ndim, pipeline_mode=pl.Buffered(1))
    return pl.pallas_call(
        _mix_body,
        grid=(SEQ // MIX_TM,),
        in_specs=[row(BRANCH_WIDTH), row(BRANCH_WIDTH), row(BRANCH_WIDTH)]
                 + [gate(n, half) for n in range(N_BRANCH) for half in range(2)]
                 + [full(b_gate), once(w_branch_b), once(w_out_b), row(D_MODEL), full(g), full(b), full(w_router_t)],
        out_specs=[row(D_MODEL), row(D_MODEL), pl.BlockSpec((N_EXPERTS, MIX_TM), lambda i: (0, i))],
        out_shape=[jax.ShapeDtypeStruct((SEQ, D_MODEL), F32), jax.ShapeDtypeStruct((SEQ, D_MODEL), F32),
                   jax.ShapeDtypeStruct((N_EXPERTS, SEQ), F32)],
        compiler_params=_params(58, 1),
        name="mix_ln",
    )(ya, yb, yc, rest, rest, rest, rest, rest, rest, b_gate, w_branch_b, w_out_b, x, g, b, w_router_t)


FINAL_TM = 512


def _final_body(y_ref, g_ref, b_ref, xo_ref, xb_ref):
    x2 = _layer_norm(y_ref[...], g_ref[...], b_ref[...])
    xo_ref[...] = x2
    xb_ref[...] = x2.astype(BF16)


def _final_ln(y, g, b):
    row = pl.BlockSpec((FINAL_TM, D_MODEL), lambda i: (i, 0))
    vec = pl.BlockSpec((1, D_MODEL), lambda i: (0, 0))
    return pl.pallas_call(
        _final_body,
        grid=(SEQ // FINAL_TM,),
        in_specs=[row, vec, vec],
        out_specs=[row, row],
        out_shape=[jax.ShapeDtypeStruct((SEQ, D_MODEL), F32), jax.ShapeDtypeStruct((SEQ, D_MODEL), BF16)],
        compiler_params=_params(48, 1),
        name="final_ln",
    )(y, g, b)


FFN_TF = 256
FFN_STEPS = D_FF // FFN_TF
GATHER_STEPS = 4
GATHER_ROWS = CAPACITY // GATHER_STEPS
GATHER_FIRST = FFN_STEPS - GATHER_STEPS
XROWS_PER_STEP = 168
XROWS_TAIL = CAPACITY - FFN_STEPS * XROWS_PER_STEP
DOWN_HALF = D_MODEL // 2


def _row_copy(x_hbm, rows_ref, sem, src_row, dst_row):
    return pltpu.make_async_copy(x_hbm.at[pl.ds(src_row, 1), :], rows_ref.at[pl.ds(dst_row, 1), :], sem)


ROW_GROUP = 8


def _start_rows(x_hbm, rows_ref, sem, idx_ref, first, count):
    def issue(g, c):
        base = pl.multiple_of(first + g * ROW_GROUP, ROW_GROUP)
        for t in range(ROW_GROUP):
            _row_copy(x_hbm, rows_ref, sem, idx_ref[0, base + t], base + t).start()
        return c
    lax.fori_loop(0, count // ROW_GROUP, issue, 0)


def _wait_rows(x_hbm, rows_ref, sem, count):
    def wait(j, c):
        _row_copy(x_hbm, rows_ref, sem, 0, 0).wait()
        return c
    lax.fori_loop(0, count, wait, 0, unroll=8)


def _ffn_body(l_ref, idx_ref, idx_next_ref, x_hbm, res_hbm, gate_ref, wg_ref, wu_ref, wd_ref, acc_hbm,
              rows_ref, xe_ref, ffn_ref, accb_ref, sem_x, sem_g, sem_s):
    del res_hbm
    e = pl.program_id(0)
    f = pl.program_id(1)
    last = f == FFN_STEPS - 1

    @pl.when((e == 0) & (f == 0))
    def _():
        _start_rows(x_hbm, rows_ref, sem_x, idx_ref, 0, CAPACITY)

    @pl.when(f == 0)
    def _():
        _wait_rows(x_hbm, rows_ref, sem_x, CAPACITY)
        _cast_into(rows_ref, xe_ref, CAPACITY)
        _start_rows(x_hbm, rows_ref, sem_x, idx_next_ref, FFN_STEPS * XROWS_PER_STEP, XROWS_TAIL)

        def zero(i, c):
            sl = pl.ds(pl.multiple_of(i * CAST_ROWS, CAST_ROWS), CAST_ROWS)
            ffn_ref[sl, :] = jnp.zeros((CAST_ROWS, D_MODEL), F32)
            return c
        lax.fori_loop(0, CAPACITY // CAST_ROWS, zero, 0)

    @pl.when((e > 0) & (f == GATHER_FIRST))
    def _():
        _wait_rows(accb_ref, acc_hbm, sem_s, CAPACITY)

    @pl.when(f >= GATHER_FIRST)
    def _():
        _start_rows(acc_hbm, accb_ref, sem_g, idx_ref, (f - GATHER_FIRST) * GATHER_ROWS, GATHER_ROWS)

    xe = xe_ref[...]
    a = jnp.dot(xe, wg_ref[...].astype(BF16), preferred_element_type=F32)
    for t in range(XROWS_PER_STEP):
        j = f * XROWS_PER_STEP + t
        _row_copy(x_hbm, rows_ref, sem_x, idx_next_ref[0, j], j).start()
    u = jnp.dot(xe, wu_ref[...].astype(BF16), preferred_element_type=F32)
    hid = (jax.nn.silu(a) * u).astype(BF16)
    for half in range(2):
        cols = slice(half * DOWN_HALF, (half + 1) * DOWN_HALF)
        ffn_ref[:, cols] += jnp.dot(hid, wd_ref[:, cols].astype(BF16), preferred_element_type=F32)

    @pl.when(last)
    def _():
        _wait_rows(acc_hbm, accb_ref, sem_g, CAPACITY)

        def update(i, c):
            sl = pl.ds(pl.multiple_of(i * CAST_ROWS, CAST_ROWS), CAST_ROWS)
            accb_ref[sl, :] += ffn_ref[sl, :] * gate_ref[sl, :]
            return c
        lax.fori_loop(0, CAPACITY // CAST_ROWS, update, 0)

        def issue(g, c):
            base = pl.multiple_of(g * ROW_GROUP, ROW_GROUP)
            for t in range(ROW_GROUP):
                _row_copy(accb_ref, acc_hbm, sem_s, base + t, idx_ref[0, base + t]).start()
            return c
        lax.fori_loop(0, CAPACITY // ROW_GROUP, issue, 0)

    @pl.when(last & (e == N_EXPERTS - 1))
    def _():
        _wait_rows(accb_ref, acc_hbm, sem_s, CAPACITY)
        _wait_rows(x_hbm, rows_ref, sem_x, CAPACITY)


def _expert_ffn_accumulate(l, idx, x1, res, gates, w_gate, w_up, w_down):
    cur = lambda e, f, l_ref: (e, 0, 0)
    nxt = lambda e, f, l_ref: (jnp.minimum(e + 1, N_EXPERTS - 1), 0, 0)
    rows = lambda dt: pltpu.VMEM((CAPACITY, D_MODEL), dt)
    return pl.pallas_call(
        _ffn_body,
        grid_spec=pltpu.PrefetchScalarGridSpec(
            num_scalar_prefetch=1,
            grid=(N_EXPERTS, FFN_STEPS),
            in_specs=[
                pl.BlockSpec((None, 1, CAPACITY), cur, memory_space=pltpu.SMEM),
                pl.BlockSpec((None, 1, CAPACITY), nxt, memory_space=pltpu.SMEM),
                pl.BlockSpec(memory_space=pl.ANY),
                pl.BlockSpec(memory_space=pl.ANY),
                pl.BlockSpec((None, CAPACITY, 1), cur),
                pl.BlockSpec((None, None, D_MODEL, FFN_TF), lambda e, f, l_ref: (l_ref[0], e, 0, f)),
                pl.BlockSpec((None, None, D_MODEL, FFN_TF), lambda e, f, l_ref: (l_ref[0], e, 0, f)),
                pl.BlockSpec((None, None, FFN_TF, D_MODEL), lambda e, f, l_ref: (l_ref[0], e, f, 0)),
            ],
            out_specs=pl.BlockSpec(memory_space=pl.ANY),
            scratch_shapes=[rows(F32), rows(BF16), rows(F32), rows(F32),
                            pltpu.SemaphoreType.DMA, pltpu.SemaphoreType.DMA, pltpu.SemaphoreType.DMA],
        ),
        out_shape=jax.ShapeDtypeStruct((SEQ, D_MODEL), F32),
        input_output_aliases={4: 0},
        compiler_params=_params(58, 2),
        name="expert_ffn",
    )(l, idx, idx, x1, res, gates, w_gate, w_up, w_down)


def kernel(x, w_in, b_gate, rpb, sink, conv_w, w_branch, w_out, ln_g, ln_b, w_router, w_gate, w_up, w_down):
    x0 = x.reshape(SEQ, D_MODEL)
    na_bias = _na_bias_tables(rpb)
    slopes = 2.0 ** (-8.0 * jnp.arange(1, WG_HEADS + 1, dtype=F32) / WG_HEADS)
    head_scalars = jnp.stack([sink, jnp.broadcast_to(slopes, sink.shape)], axis=1)
    w_router_t = jnp.swapaxes(w_router, 1, 2).astype(BF16)

    def layer(carry, per_layer):
        xf, xb = carry
        li, bias_l, hs_l, conv_l, bg_l, g_l, b_l, wr_l = per_layer
        l = li.reshape(1)
        qkv = _project(l, xb, w_in, 0, QKV_WIDTH, BF16)
        rest = _project(l, xb, w_in, QKV_WIDTH, REST_WIDTH, F32)
        ya = _neighbourhood_attention(qkv, bias_l)
        yb = _windowed_gqa(qkv, hs_l)
        yc = _short_conv(rest, conv_l)
        w_branch_b = _cast_layer(l, w_branch.reshape(DEPTH, N_BRANCH * BRANCH_WIDTH, D_MODEL), 512)
        w_out_b = _cast_layer(l, w_out, 512)
        x1, res, aff_t = _mix_ln(ya, yb, yc, rest, bg_l.reshape(1, N_BRANCH * D_MODEL), w_branch_b, w_out_b,
                                 xf, g_l[0:1], b_l[0:1], wr_l)
        gates, idx = lax.top_k(aff_t, CAPACITY)
        y = _expert_ffn_accumulate(l, idx[:, None, :], x1, res, gates[..., None], w_gate, w_up, w_down)
        x2, x2b = _final_ln(y, g_l[1:2], b_l[1:2])
        return (x2, x2b), None

    per_layer = (jnp.arange(DEPTH, dtype=jnp.int32), na_bias, head_scalars, conv_w, b_gate, ln_g, ln_b, w_router_t)
    (xf, _), _ = lax.scan(layer, (x0, _cast_rows(x0, 512)), per_layer)
    return xf.reshape(x.shape)
```

```python
import functools

import jax
import jax.numpy as jnp
from jax import lax
from jax.experimental import pallas as pl
from jax.experimental.pallas import tpu as pltpu

F32 = jnp.float32
BF16 = jnp.bfloat16

D_MODEL = 2048
SEQ = 8192
DEPTH = 4
GRID_W = 64
ROWS = SEQ // GRID_W
HEAD_DIM = 128
NA_HEADS = 8
NA_ROWS = 8
NA_COLS = 16
WG_HEADS = 8
WG_KV_HEADS = 2
WG_GROUP = WG_HEADS // WG_KV_HEADS
WG_WINDOW = 128
WG_BLOCK = 128
SC_WIDTH = 1024
BRANCH_WIDTH = 1024
N_BRANCH = 3
N_EXPERTS = 16
CAPACITY = 2 * SEQ // N_EXPERTS
D_FF = 1536
ALPHA = (2 * DEPTH) ** 0.25
LN_EPS = 1e-5
ATTN_SCALE = HEAD_DIM ** -0.5
MASKED = -1e30

QKV_WIDTH = 3 * NA_HEADS * HEAD_DIM + (WG_HEADS + 2 * WG_KV_HEADS) * HEAD_DIM
REST_WIDTH = 3 * SC_WIDTH + N_BRANCH * D_MODEL
NA_Q_BLK, NA_K_BLK, NA_V_BLK = 0, NA_HEADS, 2 * NA_HEADS
WG_Q_BLK = 3 * NA_HEADS * HEAD_DIM // (WG_GROUP * HEAD_DIM)
WG_K_BLK = (3 * NA_HEADS + WG_HEADS) * HEAD_DIM // HEAD_DIM
WG_V_BLK = WG_K_BLK + WG_KV_HEADS
GATE_COL = 3 * SC_WIDTH

MIB = 1024 * 1024


def _params(vmem_mib, n_axes):
    return pltpu.CompilerParams(dimension_semantics=("arbitrary",) * n_axes,
                                vmem_limit_bytes=vmem_mib * MIB)


def _cast_body(x_ref, o_ref):
    o_ref[...] = x_ref[...].astype(BF16)


def _cast_rows(x, tm):
    m, n = x.shape
    return pl.pallas_call(
        _cast_body,
        grid=(m // tm,),
        in_specs=[pl.BlockSpec((tm, n), lambda i: (i, 0))],
        out_specs=pl.BlockSpec((tm, n), lambda i: (i, 0)),
        out_shape=jax.ShapeDtypeStruct((m, n), BF16),
        compiler_params=_params(32, 1),
        name="cast_rows",
    )(x)


def _cast_layer_body(l_ref, x_ref, o_ref):
    o_ref[...] = x_ref[...].astype(BF16)


def _cast_layer(l, w, tm):
    _, m, n = w.shape
    return pl.pallas_call(
        _cast_layer_body,
        grid_spec=pltpu.PrefetchScalarGridSpec(
            num_scalar_prefetch=1,
            grid=(m // tm,),
            in_specs=[pl.BlockSpec((None, tm, n), lambda i, l_ref: (l_ref[0], i, 0))],
            out_specs=pl.BlockSpec((tm, n), lambda i, l_ref: (i, 0)),
        ),
        out_shape=jax.ShapeDtypeStruct((m, n), BF16),
        compiler_params=_params(32, 1),
        name="cast_layer",
    )(l, w)


def _cast_branch(l, w, tm):
    _, nb, m, n = w.shape
    per = m // tm
    return pl.pallas_call(
        _cast_layer_body,
        grid_spec=pltpu.PrefetchScalarGridSpec(
            num_scalar_prefetch=1,
            grid=(nb, per),
            in_specs=[pl.BlockSpec((None, None, tm, n), lambda b, i, l_ref: (l_ref[0], b, i, 0))],
            out_specs=pl.BlockSpec((tm, n), lambda b, i, l_ref: (b * per + i, 0)),
        ),
        out_shape=jax.ShapeDtypeStruct((nb * m, n), BF16),
        compiler_params=_params(32, 2),
        name="cast_branch",
    )(l, w)


PROJ_TM = 1024
PROJ_TN = 1536
CAST_ROWS = 256


def _cast_into(w_ref, wb_ref, rows):
    def chunk(i, c):
        sl = pl.ds(pl.multiple_of(i * CAST_ROWS, CAST_ROWS), CAST_ROWS)
        wb_ref[sl, :] = w_ref[sl, :].astype(BF16)
        return c
    lax.fori_loop(0, rows // CAST_ROWS, chunk, 0)


def _proj_body(l_ref, x_ref, w_ref, o_ref, wb_ref):
    @pl.when(pl.program_id(1) == 0)
    def _():
        _cast_into(w_ref, wb_ref, D_MODEL)

    o_ref[...] = jnp.dot(x_ref[...], wb_ref[...], preferred_element_type=F32).astype(o_ref.dtype)


def _project(l, xb, w_in, col0, width, out_dtype):
    n_tiles, off = width // PROJ_TN, col0 // PROJ_TN
    return pl.pallas_call(
        _proj_body,
        grid_spec=pltpu.PrefetchScalarGridSpec(
            num_scalar_prefetch=1,
            grid=(n_tiles, SEQ // PROJ_TM),
            in_specs=[
                pl.BlockSpec((PROJ_TM, D_MODEL), lambda j, i, l_ref: (i, 0)),
                pl.BlockSpec((None, D_MODEL, PROJ_TN), lambda j, i, l_ref: (l_ref[0], 0, off + j)),
            ],
            out_specs=pl.BlockSpec((PROJ_TM, PROJ_TN), lambda j, i, l_ref: (i, j)),
            scratch_shapes=[pltpu.VMEM((D_MODEL, PROJ_TN), BF16)],
        ),
        out_shape=jax.ShapeDtypeStruct((SEQ, width), out_dtype),
        compiler_params=_params(58, 2),
        name="in_proj",
    )(l, xb, w_in)


NA_CHUNK_ROWS = 32
NA_GROUP = 16
NA_KEYS = NA_ROWS * GRID_W


def _na_bias_tables(rpb):
    c = jnp.arange(GRID_W)[:, None]
    cj = jnp.arange(GRID_W)[None, :]
    cs = jnp.clip(c - NA_COLS // 2, 0, GRID_W - NA_COLS)
    valid = (cj >= cs) & (cj < cs + NA_COLS)
    select = ((cj - c + (NA_COLS - 1))[None] == jnp.arange(2 * NA_COLS - 1)[:, None, None]) & valid[None]
    t = jnp.einsum('lhrk,kcj->lhcrj', rpb.astype(F32), select.astype(F32),
                   precision=lax.Precision.HIGHEST)
    t = jnp.where(valid[None, None, :, None, :], t, MASKED)
    per_d = [t[:, :, :, NA_ROWS - 1 - d:2 * NA_ROWS - 1 - d, :].reshape(DEPTH, NA_HEADS, GRID_W, NA_KEYS)
             for d in range(NA_ROWS)]
    return jnp.stack(per_d, axis=1)


def _na_body(q_ref, k_ref, v_ref, bias_ref, o_ref):
    chunk = pl.program_id(1)

    def group(gi, carry):
        qss, kss, logits = [], [], []
        for t in range(NA_GROUP):
            i = gi * NA_GROUP + t
            r = chunk * NA_CHUNK_ROWS + i
            rs = jnp.clip(r - NA_ROWS // 2, 0, ROWS - NA_ROWS)
            qs = pl.ds(pl.multiple_of(i * GRID_W, GRID_W), GRID_W)
            ks = pl.ds(pl.multiple_of(rs * GRID_W, GRID_W), NA_KEYS)
            s = lax.dot_general(q_ref[qs, :], k_ref[ks, :], (((1,), (1,)), ((), ())),
                                preferred_element_type=F32)
            logits.append(s * ATTN_SCALE + bias_ref[r - rs])
            qss.append(qs)
            kss.append(ks)
        probs, dens = [], []
        for s in logits:
            e = jnp.exp(s - jnp.max(s, axis=-1, keepdims=True))
            dens.append(jnp.sum(e, axis=-1, keepdims=True))
            probs.append(e.astype(BF16))
        for qs, ks, p, den in zip(qss, kss, probs, dens):
            o = jnp.dot(p, v_ref[ks, :], preferred_element_type=F32) / den
            o_ref[qs, :] = o.astype(o_ref.dtype)
        return carry

    lax.fori_loop(0, NA_CHUNK_ROWS // NA_GROUP, group, 0)


def _neighbourhood_attention(qkv, bias):
    tq = NA_CHUNK_ROWS * GRID_W
    return pl.pallas_call(
        _na_body,
        grid=(NA_HEADS, ROWS // NA_CHUNK_ROWS),
        in_specs=[
            pl.BlockSpec((tq, HEAD_DIM), lambda h, c: (c, NA_Q_BLK + h)),
            pl.BlockSpec((SEQ, HEAD_DIM), lambda h, c: (0, NA_K_BLK + h)),
            pl.BlockSpec((SEQ, HEAD_DIM), lambda h, c: (0, NA_V_BLK + h)),
            pl.BlockSpec((NA_ROWS, None, GRID_W, NA_KEYS), lambda h, c: (0, h, 0, 0)),
        ],
        out_specs=pl.BlockSpec((tq, HEAD_DIM), lambda h, c: (c, h)),
        out_shape=jax.ShapeDtypeStruct((SEQ, NA_HEADS * HEAD_DIM), BF16),
        compiler_params=_params(32, 2),
        name="na_attn",
    )(qkv, qkv, qkv, bias)


WG_CHUNK_BLOCKS = 8
WG_PAIR = 2
WG_KEYS = 3 * WG_BLOCK


def _wg_body(hs_ref, q_ref, k_ref, v_ref, o_ref):
    kv = pl.program_id(0)
    chunk = pl.program_id(1)

    heads = [(slice(g * HEAD_DIM, (g + 1) * HEAD_DIM), hs_ref[0, kv * WG_GROUP + g], hs_ref[1, kv * WG_GROUP + g])
             for g in range(WG_GROUP)]

    def blocks(jj, carry):
        work, logits = [], []
        for t in range(WG_PAIR):
            j = jj * WG_PAIR + t
            n = chunk * WG_CHUNK_BLOCKS + j
            start = jnp.clip((n - 1) * WG_BLOCK, 0, SEQ - WG_KEYS)
            ks = pl.ds(pl.multiple_of(start, WG_BLOCK), WG_KEYS)
            qs = pl.ds(pl.multiple_of(j * WG_BLOCK, WG_BLOCK), WG_BLOCK)
            kw = k_ref[ks, :]
            q_pos = n * WG_BLOCK + lax.broadcasted_iota(jnp.int32, (WG_BLOCK, WG_KEYS), 0)
            k_pos = start + lax.broadcasted_iota(jnp.int32, (WG_BLOCK, WG_KEYS), 1)
            dist = jnp.abs(k_pos - q_pos)
            valid = dist <= WG_WINDOW
            dist_f = dist.astype(F32)
            for cols, sink, slope in heads:
                s = lax.dot_general(q_ref[qs, cols], kw, (((1,), (1,)), ((), ())), preferred_element_type=F32)
                logits.append(jnp.where(valid, s * ATTN_SCALE - slope * dist_f, MASKED))
                work.append((qs, ks, cols, sink))
        probs, dens = [], []
        for s, (qs, ks, cols, sink) in zip(logits, work):
            m = jnp.maximum(jnp.max(s, axis=-1, keepdims=True), sink)
            e = jnp.exp(s - m)
            dens.append(jnp.sum(e, axis=-1, keepdims=True) + jnp.exp(sink - m))
            probs.append(e.astype(BF16))
        for p, den, (qs, ks, cols, sink) in zip(probs, dens, work):
            o = jnp.dot(p, v_ref[ks, :], preferred_element_type=F32) / den
            o_ref[qs, cols] = o.astype(o_ref.dtype)
        return carry

    lax.fori_loop(0, WG_CHUNK_BLOCKS // WG_PAIR, blocks, 0)


def _windowed_gqa(qkv, head_scalars):
    tq = WG_CHUNK_BLOCKS * WG_BLOCK
    gw = WG_GROUP * HEAD_DIM
    return pl.pallas_call(
        _wg_body,
        grid=(WG_KV_HEADS, SEQ // tq),
        in_specs=[
            pl.BlockSpec(memory_space=pltpu.SMEM),
            pl.BlockSpec((tq, gw), lambda k, c: (c, WG_Q_BLK + k)),
            pl.BlockSpec((SEQ, HEAD_DIM), lambda k, c: (0, WG_K_BLK + k)),
            pl.BlockSpec((SEQ, HEAD_DIM), lambda k, c: (0, WG_V_BLK + k)),
        ],
        out_specs=pl.BlockSpec((tq, gw), lambda k, c: (c, k)),
        out_shape=jax.ShapeDtypeStruct((SEQ, WG_HEADS * HEAD_DIM), BF16),
        compiler_params=_params(32, 2),
        name="wg_attn",
    )(head_scalars, qkv, qkv, qkv)


CONV_TM = 512
HALO = 8


def _conv_body(bg_ref, cg_ref, hc_ref, cgp_ref, hcp_ref, cgn_ref, hcn_ref, w_ref, o_ref):
    i = pl.program_id(0)
    u = cg_ref[...] * hc_ref[...]
    prev_row = cgp_ref[HALO - 1:HALO, :] * hcp_ref[HALO - 1:HALO, :]
    next_row = cgn_ref[0:1, :] * hcn_ref[0:1, :]
    prev_row = jnp.where(i == 0, 0.0, prev_row)
    next_row = jnp.where(i == pl.num_programs(0) - 1, 0.0, next_row)
    row = lax.broadcasted_iota(jnp.int32, u.shape, 0)
    u_prev = jnp.where(row == 0, prev_row, pltpu.roll(u, 1, 0))
    u_next = jnp.where(row == CONV_TM - 1, next_row, pltpu.roll(u, CONV_TM - 1, 0))
    y = bg_ref[...] * (w_ref[0:1, :] * u_prev + w_ref[1:2, :] * u + w_ref[2:3, :] * u_next)
    o_ref[...] = y.astype(o_ref.dtype)


def _short_conv(rest, conv_w):
    nb = CONV_TM // HALO
    last = SEQ // HALO - 1
    main = lambda col: pl.BlockSpec((CONV_TM, SC_WIDTH), lambda i: (i, col))
    prev = lambda col: pl.BlockSpec((HALO, SC_WIDTH), lambda i: (jnp.maximum(i * nb - 1, 0), col))
    nxt = lambda col: pl.BlockSpec((HALO, SC_WIDTH), lambda i: (jnp.minimum((i + 1) * nb, last), col))
    return pl.pallas_call(
        _conv_body,
        grid=(SEQ // CONV_TM,),
        in_specs=[main(0), main(1), main(2), prev(1), prev(2), nxt(1), nxt(2),
                  pl.BlockSpec((3, SC_WIDTH), lambda i: (0, 0))],
        out_specs=pl.BlockSpec((CONV_TM, SC_WIDTH), lambda i: (i, 0)),
        out_shape=jax.ShapeDtypeStruct((SEQ, SC_WIDTH), BF16),
        compiler_params=_params(40, 1),
        name="short_conv",
    )(rest, rest, rest, rest, rest, rest, rest, conv_w)


MERGE_TM = 512
MERGE_TN = 512


def _merge_body(l_ref, ya_ref, yb_ref, yc_ref, g0_ref, g1_ref, g2_ref, b0_ref, b1_ref, b2_ref, w_ref,
                o_ref, wb_ref):
    @pl.when(pl.program_id(1) == 0)
    def _():
        for n in range(N_BRANCH):
            _cast_into(w_ref.at[n], wb_ref.at[n], BRANCH_WIDTH)

    acc = jnp.zeros((MERGE_TM, MERGE_TN), F32)
    for n, (y_ref, g_ref, b_ref) in enumerate(((ya_ref, g0_ref, b0_ref), (yb_ref, g1_ref, b1_ref),
                                               (yc_ref, g2_ref, b2_ref))):
        branch = jnp.dot(y_ref[...], wb_ref[n], preferred_element_type=F32)
        acc = acc + jax.nn.sigmoid(g_ref[...] + b_ref[...]) * branch
    o_ref[...] = acc.astype(o_ref.dtype)


def _merge(l, ya, yb, yc, rest, b_gate, w_branch):
    y_spec = pl.BlockSpec((MERGE_TM, BRANCH_WIDTH), lambda j, i, l_ref: (i, 0))
    gcol = lambda n: (GATE_COL + n * D_MODEL) // MERGE_TN
    g_spec = lambda n: pl.BlockSpec((MERGE_TM, MERGE_TN), lambda j, i, l_ref: (i, gcol(n) + j))
    b_spec = lambda n: pl.BlockSpec((1, MERGE_TN), lambda j, i, l_ref: (0, n * D_MODEL // MERGE_TN + j))
    return pl.pallas_call(
        _merge_body,
        grid_spec=pltpu.PrefetchScalarGridSpec(
            num_scalar_prefetch=1,
            grid=(D_MODEL // MERGE_TN, SEQ // MERGE_TM),
            in_specs=[y_spec, y_spec, y_spec, g_spec(0), g_spec(1), g_spec(2), b_spec(0), b_spec(1), b_spec(2),
                      pl.BlockSpec((None, N_BRANCH, BRANCH_WIDTH, MERGE_TN),
                                   lambda j, i, l_ref: (l_ref[0], 0, 0, j))],
            out_specs=pl.BlockSpec((MERGE_TM, MERGE_TN), lambda j, i, l_ref: (i, j)),
            scratch_shapes=[pltpu.VMEM((N_BRANCH, BRANCH_WIDTH, MERGE_TN), BF16)],
        ),
        out_shape=jax.ShapeDtypeStruct((SEQ, D_MODEL), BF16),
        compiler_params=_params(48, 2),
        name="branch_merge",
    )(l, ya, yb, yc, rest, rest, rest, b_gate, b_gate, b_gate, w_branch)


def _layer_norm(y, g, b):
    mu = jnp.mean(y, axis=-1, keepdims=True)
    yc = y - mu
    var = jnp.mean(yc * yc, axis=-1, keepdims=True)
    return yc * lax.rsqrt(var + LN_EPS) * g + b


OUT_TM = 256


def _out_body(m_ref, w_ref, x_ref, g_ref, b_ref, wr_ref, xo_ref, res_ref, aff_ref):
    mix = jnp.dot(m_ref[...], w_ref[...], preferred_element_type=F32)
    x1 = _layer_norm(ALPHA * x_ref[...] + mix, g_ref[...], b_ref[...])
    xo_ref[...] = x1
    res_ref[...] = ALPHA * x1
    logits = lax.dot_general(wr_ref[...], x1.astype(BF16), (((1,), (1,)), ((), ())),
                             preferred_element_type=F32)
    e = jnp.exp(logits - jnp.max(logits, axis=0, keepdims=True))
    aff_ref[...] = e / jnp.sum(e, axis=0, keepdims=True)


def _out_proj_ln(merged, w_out_b, x, g, b, w_router_t):
    row = pl.BlockSpec((OUT_TM, D_MODEL), lambda i: (i, 0))
    full = lambda a: pl.BlockSpec(a.shape, lambda i: (0,) * a.ndim)
    return pl.pallas_call(
        _out_body,
        grid=(SEQ // OUT_TM,),
        in_specs=[row, full(w_out_b), row, full(g), full(b), full(w_router_t)],
        out_specs=[row, row, pl.BlockSpec((N_EXPERTS, OUT_TM), lambda i: (0, i))],
        out_shape=[jax.ShapeDtypeStruct((SEQ, D_MODEL), F32), jax.ShapeDtypeStruct((SEQ, D_MODEL), F32),
                   jax.ShapeDtypeStruct((N_EXPERTS, SEQ), F32)],
        compiler_params=_params(48, 1),
        name="out_proj_ln",
    )(merged, w_out_b, x, g, b, w_router_t)


MIX_TM = 256
GATE_TN = 1024


def _mix_body(ya_ref, yb_ref, yc_ref, g00, g01, g10, g11, g20, g21, bg_ref, wb_ref, wo_ref, x_ref, g_ref, b_ref,
              wr_ref, xo_ref, res_ref, aff_ref):
    y_refs = (ya_ref, yb_ref, yc_ref)
    gate_refs = ((g00, g01), (g10, g11), (g20, g21))
    halves = []
    for half in range(D_MODEL // GATE_TN):
        acc = jnp.zeros((MIX_TM, GATE_TN), F32)
        for n in range(N_BRANCH):
            col0 = n * D_MODEL + half * GATE_TN
            branch = jnp.dot(y_refs[n][...],
                             wb_ref[n * BRANCH_WIDTH:(n + 1) * BRANCH_WIDTH, half * GATE_TN:(half + 1) * GATE_TN],
                             preferred_element_type=F32)
            gate = jax.nn.sigmoid(gate_refs[n][half][...] + bg_ref[:, col0:col0 + GATE_TN])
            acc = acc + gate * branch
        halves.append(acc.astype(BF16))
    mix = sum(jnp.dot(m, wo_ref[h * GATE_TN:(h + 1) * GATE_TN, :], preferred_element_type=F32)
              for h, m in enumerate(halves))
    x1 = _layer_norm(ALPHA * x_ref[...] + mix, g_ref[...], b_ref[...])
    xo_ref[...] = x1
    res_ref[...] = ALPHA * x1
    logits = lax.dot_general(wr_ref[...], x1.astype(BF16), (((1,), (1,)), ((), ())),
                             preferred_element_type=F32)
    e = jnp.exp(logits - jnp.max(logits, axis=0, keepdims=True))
    aff_ref[...] = e / jnp.sum(e, axis=0, keepdims=True)


def _mix_ln(ya, yb, yc, rest, b_gate, w_branch_b, w_out_b, x, g, b, w_router_t):
    row = lambda n: pl.BlockSpec((MIX_TM, n), lambda i: (i, 0))
    gate = lambda n, half: pl.BlockSpec(
        (MIX_TM, GATE_TN), lambda i: (i, (GATE_COL + n * D_MODEL) // GATE_TN + half))
    full = lambda a: pl.BlockSpec(a.shape, lambda i: (0,) * a.ndim)
    once = lambda a: pl.BlockSpec(a.shape, lambda i: (0,) * a.ndim, pipeline_mode=pl.Buffered(1))
    return pl.pallas_call(
        _mix_body,
        grid=(SEQ // MIX_TM,),
        in_specs=[row(BRANCH_WIDTH), row(BRANCH_WIDTH), row(BRANCH_WIDTH)]
                 + [gate(n, half) for n in range(N_BRANCH) for half in range(2)]
                 + [full(b_gate), once(w_branch_b), once(w_out_b), row(D_MODEL), full(g), full(b), full(w_router_t)],
        out_specs=[row(D_MODEL), row(D_MODEL), pl.BlockSpec((N_EXPERTS, MIX_TM), lambda i: (0, i))],
        out_shape=[jax.ShapeDtypeStruct((SEQ, D_MODEL), F32), jax.ShapeDtypeStruct((SEQ, D_MODEL), F32),
                   jax.ShapeDtypeStruct((N_EXPERTS, SEQ), F32)],
        compiler_params=_params(58, 1),
        name="mix_ln",
    )(ya, yb, yc, rest, rest, rest, rest, rest, rest, b_gate, w_branch_b, w_out_b, x, g, b, w_router_t)


FINAL_TM = 512


def _final_body(y_ref, g_ref, b_ref, xo_ref, xb_ref):
    x2 = _layer_norm(y_ref[...], g_ref[...], b_ref[...])
    xo_ref[...] = x2
    xb_ref[...] = x2.astype(BF16)


def _final_ln(y, g, b):
    row = pl.BlockSpec((FINAL_TM, D_MODEL), lambda i: (i, 0))
    vec = pl.BlockSpec((1, D_MODEL), lambda i: (0, 0))
    return pl.pallas_call(
        _final_body,
        grid=(SEQ // FINAL_TM,),
        in_specs=[row, vec, vec],
        out_specs=[row, row],
        out_shape=[jax.ShapeDtypeStruct((SEQ, D_MODEL), F32), jax.ShapeDtypeStruct((SEQ, D_MODEL), BF16)],
        compiler_params=_params(48, 1),
        name="final_ln",
    )(y, g, b)


FFN_TF = 256
FFN_STEPS = D_FF // FFN_TF
GATHER_STEPS = 4
GATHER_ROWS = CAPACITY // GATHER_STEPS
GATHER_FIRST = FFN_STEPS - GATHER_STEPS
XROWS_PER_STEP = 168
XROWS_TAIL = CAPACITY - FFN_STEPS * XROWS_PER_STEP
DOWN_HALF = D_MODEL // 2


def _row_copy(x_hbm, rows_ref, sem, src_row, dst_row):
    return pltpu.make_async_copy(x_hbm.at[pl.ds(src_row, 1), :], rows_ref.at[pl.ds(dst_row, 1), :], sem)


ROW_GROUP = 8


def _start_rows(x_hbm, rows_ref, sem, idx_ref, first, count):
    def issue(g, c):
        base = pl.multiple_of(first + g * ROW_GROUP, ROW_GROUP)
        for t in range(ROW_GROUP):
            _row_copy(x_hbm, rows_ref, sem, idx_ref[0, base + t], base + t).start()
        return c
    lax.fori_loop(0, count // ROW_GROUP, issue, 0)


def _wait_rows(x_hbm, rows_ref, sem, count):
    def wait(j, c):
        _row_copy(x_hbm, rows_ref, sem, 0, 0).wait()
        return c
    lax.fori_loop(0, count, wait, 0, unroll=8)


def _ffn_body(l_ref, idx_ref, idx_next_ref, x_hbm, res_hbm, gate_ref, wg_ref, wu_ref, wd_ref, acc_hbm,
              rows_ref, xe_ref, ffn_ref, accb_ref, sem_x, sem_g, sem_s):
    del res_hbm
    e = pl.program_id(0)
    f = pl.program_id(1)
    last = f == FFN_STEPS - 1

    @pl.when((e == 0) & (f == 0))
    def _():
        _start_rows(x_hbm, rows_ref, sem_x, idx_ref, 0, CAPACITY)

    @pl.when(f == 0)
    def _():
        _wait_rows(x_hbm, rows_ref, sem_x, CAPACITY)
        _cast_into(rows_ref, xe_ref, CAPACITY)
        _start_rows(x_hbm, rows_ref, sem_x, idx_next_ref, FFN_STEPS * XROWS_PER_STEP, XROWS_TAIL)

        def zero(i, c):
            sl = pl.ds(pl.multiple_of(i * CAST_ROWS, CAST_ROWS), CAST_ROWS)
            ffn_ref[sl, :] = jnp.zeros((CAST_ROWS, D_MODEL), F32)
            return c
        lax.fori_loop(0, CAPACITY // CAST_ROWS, zero, 0)

    @pl.when((e > 0) & (f == GATHER_FIRST))
    def _():
        _wait_rows(accb_ref, acc_hbm, sem_s, CAPACITY)

    def step(k):
        for t in range(XROWS_PER_STEP):
            j = k * XROWS_PER_STEP + t
            _row_copy(x_hbm, rows_ref, sem_x, idx_next_ref[0, j], j).start()
        if k >= GATHER_FIRST:
            for t in range(GATHER_ROWS):
                j = (k - GATHER_FIRST) * GATHER_ROWS + t
                _row_copy(acc_hbm, accb_ref, sem_g, idx_ref[0, j], j).start()
        xe = xe_ref[...]
        a = jnp.dot(xe, wg_ref[...].astype(BF16), preferred_element_type=F32)
        u = jnp.dot(xe, wu_ref[...].astype(BF16), preferred_element_type=F32)
        hid = (jax.nn.silu(a) * u).astype(BF16)
        for half in range(2):
            cols = slice(half * DOWN_HALF, (half + 1) * DOWN_HALF)
            ffn_ref[:, cols] += jnp.dot(hid, wd_ref[:, cols].astype(BF16), preferred_element_type=F32)

    for k in range(FFN_STEPS):
        pl.when(f == k)(functools.partial(step, k))

    @pl.when(last)
    def _():
        _wait_rows(acc_hbm, accb_ref, sem_g, CAPACITY)

        def update(i, c):
            sl = pl.ds(pl.multiple_of(i * CAST_ROWS, CAST_ROWS), CAST_ROWS)
            accb_ref[sl, :] += ffn_ref[sl, :] * gate_ref[sl, :]
            return c
        lax.fori_loop(0, CAPACITY // CAST_ROWS, update, 0)

        def issue(g, c):
            base = pl.multiple_of(g * ROW_GROUP, ROW_GROUP)
            for t in range(ROW_GROUP):
                _row_copy(accb_ref, acc_hbm, sem_s, base + t, idx_ref[0, base + t]).start()
            return c
        lax.fori_loop(0, CAPACITY // ROW_GROUP, issue, 0)

    @pl.when(last & (e == N_EXPERTS - 1))
    def _():
        _wait_rows(accb_ref, acc_hbm, sem_s, CAPACITY)
        _wait_rows(x_hbm, rows_ref, sem_x, CAPACITY)


def _expert_ffn_accumulate(l, idx, x1, res, gates, w_gate, w_up, w_down):
    cur = lambda e, f, l_ref: (e, 0, 0)
    nxt = lambda e, f, l_ref: (jnp.minimum(e + 1, N_EXPERTS - 1), 0, 0)
    rows = lambda dt: pltpu.VMEM((CAPACITY, D_MODEL), dt)
    return pl.pallas_call(
        _ffn_body,
        grid_spec=pltpu.PrefetchScalarGridSpec(
            num_scalar_prefetch=1,
            grid=(N_EXPERTS, FFN_STEPS),
            in_specs=[
                pl.BlockSpec((None, 1, CAPACITY), cur, memory_space=pltpu.SMEM),
                pl.BlockSpec((None, 1, CAPACITY), nxt, memory_space=pltpu.SMEM),
                pl.BlockSpec(memory_space=pl.ANY),
                pl.BlockSpec(memory_space=pl.ANY),
                pl.BlockSpec((None, CAPACITY, 1), cur),
                pl.BlockSpec((None, None, D_MODEL, FFN_TF), lambda e, f, l_ref: (l_ref[0], e, 0, f)),
                pl.BlockSpec((None, None, D_MODEL, FFN_TF), lambda e, f, l_ref: (l_ref[0], e, 0, f)),
                pl.BlockSpec((None, None, FFN_TF, D_MODEL), lambda e, f, l_ref: (l_ref[0], e, f, 0)),
            ],
            out_specs=pl.BlockSpec(memory_space=pl.ANY),
            scratch_shapes=[rows(F32), rows(BF16), rows(F32), rows(F32),
                            pltpu.SemaphoreType.DMA, pltpu.SemaphoreType.DMA, pltpu.SemaphoreType.DMA],
        ),
        out_shape=jax.ShapeDtypeStruct((SEQ, D_MODEL), F32),
        input_output_aliases={4: 0},
        compiler_params=_params(58, 2),
        name="expert_ffn",
    )(l, idx, idx, x1, res, gates, w_gate, w_up, w_down)


def kernel(x, w_in, b_gate, rpb, sink, conv_w, w_branch, w_out, ln_g, ln_b, w_router, w_gate, w_up, w_down):
    x0 = x.reshape(SEQ, D_MODEL)
    na_bias = _na_bias_tables(rpb)
    slopes = 2.0 ** (-8.0 * jnp.arange(1, WG_HEADS + 1, dtype=F32) / WG_HEADS)
    head_scalars = jnp.stack([sink, jnp.broadcast_to(slopes, sink.shape)], axis=1)
    w_router_t = jnp.swapaxes(w_router, 1, 2).astype(BF16)

    def layer(carry, per_layer):
        xf, xb = carry
        li, bias_l, hs_l, conv_l, bg_l, g_l, b_l, wr_l = per_layer
        l = li.reshape(1)
        qkv = _project(l, xb, w_in, 0, QKV_WIDTH, BF16)
        rest = _project(l, xb, w_in, QKV_WIDTH, REST_WIDTH, F32)
        ya = _neighbourhood_attention(qkv, bias_l)
        yb = _windowed_gqa(qkv, hs_l)
        yc = _short_conv(rest, conv_l)
        w_branch_b = _cast_branch(l, w_branch, 512)
        w_out_b = _cast_layer(l, w_out, 512)
        x1, res, aff_t = _mix_ln(ya, yb, yc, rest, bg_l.reshape(1, N_BRANCH * D_MODEL), w_branch_b, w_out_b,
                                 xf, g_l[0:1], b_l[0:1], wr_l)
        gates, idx = lax.top_k(aff_t, CAPACITY)
        y = _expert_ffn_accumulate(l, idx[:, None, :], x1, res, gates[..., None], w_gate, w_up, w_down)
        x2, x2b = _final_ln(y, g_l[1:2], b_l[1:2])
        return (x2, x2b), None

    per_layer = (jnp.arange(DEPTH, dtype=jnp.int32), na_bias, head_scalars, conv_w, b_gate, ln_g, ln_b, w_router_t)
    (xf, _), _ = lax.scan(layer, (x0, _cast_rows(x0, 512)), per_layer)
    return xf.reshape(x.shape)
```

```python
import functools

import jax
import jax.numpy as jnp
from jax import lax
from jax.experimental import pallas as pl
from jax.experimental.pallas import tpu as pltpu

F32 = jnp.float32
BF16 = jnp.bfloat16

D_MODEL = 2048
SEQ = 8192
DEPTH = 4
GRID_W = 64
ROWS = SEQ // GRID_W
HEAD_DIM = 128
NA_HEADS = 8
NA_ROWS = 8
NA_COLS = 16
WG_HEADS = 8
WG_KV_HEADS = 2
WG_GROUP = WG_HEADS // WG_KV_HEADS
WG_WINDOW = 128
WG_BLOCK = 128
SC_WIDTH = 1024
BRANCH_WIDTH = 1024
N_BRANCH = 3
N_EXPERTS = 16
CAPACITY = 2 * SEQ // N_EXPERTS
D_FF = 1536
ALPHA = (2 * DEPTH) ** 0.25
LN_EPS = 1e-5
ATTN_SCALE = HEAD_DIM ** -0.5
MASKED = -1e30

QKV_WIDTH = 3 * NA_HEADS * HEAD_DIM + (WG_HEADS + 2 * WG_KV_HEADS) * HEAD_DIM
REST_WIDTH = 3 * SC_WIDTH + N_BRANCH * D_MODEL
NA_Q_BLK, NA_K_BLK, NA_V_BLK = 0, NA_HEADS, 2 * NA_HEADS
WG_Q_BLK = 3 * NA_HEADS * HEAD_DIM // (WG_GROUP * HEAD_DIM)
WG_K_BLK = (3 * NA_HEADS + WG_HEADS) * HEAD_DIM // HEAD_DIM
WG_V_BLK = WG_K_BLK + WG_KV_HEADS
GATE_COL = 3 * SC_WIDTH

MIB = 1024 * 1024


def _params(vmem_mib, n_axes):
    return pltpu.CompilerParams(dimension_semantics=("arbitrary",) * n_axes,
                                vmem_limit_bytes=vmem_mib * MIB)


def _cast_body(x_ref, o_ref):
    o_ref[...] = x_ref[...].astype(BF16)


def _cast_rows(x, tm):
    m, n = x.shape
    return pl.pallas_call(
        _cast_body,
        grid=(m // tm,),
        in_specs=[pl.BlockSpec((tm, n), lambda i: (i, 0))],
        out_specs=pl.BlockSpec((tm, n), lambda i: (i, 0)),
        out_shape=jax.ShapeDtypeStruct((m, n), BF16),
        compiler_params=_params(32, 1),
        name="cast_rows",
    )(x)


def _cast_layer_body(l_ref, x_ref, o_ref):
    o_ref[...] = x_ref[...].astype(BF16)


def _cast_layer(l, w, tm):
    _, m, n = w.shape
    return pl.pallas_call(
        _cast_layer_body,
        grid_spec=pltpu.PrefetchScalarGridSpec(
            num_scalar_prefetch=1,
            grid=(m // tm,),
            in_specs=[pl.BlockSpec((None, tm, n), lambda i, l_ref: (l_ref[0], i, 0))],
            out_specs=pl.BlockSpec((tm, n), lambda i, l_ref: (i, 0)),
        ),
        out_shape=jax.ShapeDtypeStruct((m, n), BF16),
        compiler_params=_params(32, 1),
        name="cast_layer",
    )(l, w)


def _cast_branch(l, w, tm):
    _, nb, m, n = w.shape
    per = m // tm
    return pl.pallas_call(
        _cast_layer_body,
        grid_spec=pltpu.PrefetchScalarGridSpec(
            num_scalar_prefetch=1,
            grid=(nb, per),
            in_specs=[pl.BlockSpec((None, None, tm, n), lambda b, i, l_ref: (l_ref[0], b, i, 0))],
            out_specs=pl.BlockSpec((tm, n), lambda b, i, l_ref: (b * per + i, 0)),
        ),
        out_shape=jax.ShapeDtypeStruct((nb * m, n), BF16),
        compiler_params=_params(32, 2),
        name="cast_branch",
    )(l, w)


PROJ_TM = 1024
PROJ_TN = 1536
CAST_ROWS = 256


def _cast_into(w_ref, wb_ref, rows):
    def chunk(i, c):
        sl = pl.ds(pl.multiple_of(i * CAST_ROWS, CAST_ROWS), CAST_ROWS)
        wb_ref[sl, :] = w_ref[sl, :].astype(BF16)
        return c
    lax.fori_loop(0, rows // CAST_ROWS, chunk, 0)


def _proj_body(l_ref, x_ref, w_ref, o_ref, wb_ref):
    @pl.when(pl.program_id(1) == 0)
    def _():
        _cast_into(w_ref, wb_ref, D_MODEL)

    o_ref[...] = jnp.dot(x_ref[...], wb_ref[...], preferred_element_type=F32).astype(o_ref.dtype)


def _project(l, xb, w_in, col0, width, out_dtype):
    n_tiles, off = width // PROJ_TN, col0 // PROJ_TN
    return pl.pallas_call(
        _proj_body,
        grid_spec=pltpu.PrefetchScalarGridSpec(
            num_scalar_prefetch=1,
            grid=(n_tiles, SEQ // PROJ_TM),
            in_specs=[
                pl.BlockSpec((PROJ_TM, D_MODEL), lambda j, i, l_ref: (i, 0)),
                pl.BlockSpec((None, D_MODEL, PROJ_TN), lambda j, i, l_ref: (l_ref[0], 0, off + j)),
            ],
            out_specs=pl.BlockSpec((PROJ_TM, PROJ_TN), lambda j, i, l_ref: (i, j)),
            scratch_shapes=[pltpu.VMEM((D_MODEL, PROJ_TN), BF16)],
        ),
        out_shape=jax.ShapeDtypeStruct((SEQ, width), out_dtype),
        compiler_params=_params(58, 2),
        name="in_proj",
    )(l, xb, w_in)


NA_CHUNK_ROWS = 32
NA_GROUP = 16
NA_KEYS = NA_ROWS * GRID_W


def _na_bias_tables(rpb):
    c = jnp.arange(GRID_W)[:, None]
    cj = jnp.arange(GRID_W)[None, :]
    cs = jnp.clip(c - NA_COLS // 2, 0, GRID_W - NA_COLS)
    valid = (cj >= cs) & (cj < cs + NA_COLS)
    select = ((cj - c + (NA_COLS - 1))[None] == jnp.arange(2 * NA_COLS - 1)[:, None, None]) & valid[None]
    t = jnp.einsum('lhrk,kcj->lhrcj', rpb.astype(F32), select.astype(F32),
                   precision=lax.Precision.HIGHEST)
    d = jnp.arange(NA_ROWS)[:, None, None]
    i = jnp.arange(NA_ROWS)[None, :, None]
    row_select = (i - d + (NA_ROWS - 1)) == jnp.arange(2 * NA_ROWS - 1)[None, None, :]
    t = jnp.einsum('dir,lhrcj->ldhcij', row_select.astype(F32), t, precision=lax.Precision.HIGHEST)
    t = jnp.where(valid[None, None, None, :, None, :], t, MASKED)
    return t.reshape(DEPTH, NA_ROWS, NA_HEADS, GRID_W, NA_KEYS)


def _na_body(q_ref, k_ref, v_ref, bias_ref, o_ref):
    chunk = pl.program_id(1)

    def group(gi, carry):
        qss, kss, logits = [], [], []
        for t in range(NA_GROUP):
            i = gi * NA_GROUP + t
            r = chunk * NA_CHUNK_ROWS + i
            rs = jnp.clip(r - NA_ROWS // 2, 0, ROWS - NA_ROWS)
            qs = pl.ds(pl.multiple_of(i * GRID_W, GRID_W), GRID_W)
            ks = pl.ds(pl.multiple_of(rs * GRID_W, GRID_W), NA_KEYS)
            s = lax.dot_general(q_ref[qs, :], k_ref[ks, :], (((1,), (1,)), ((), ())),
                                preferred_element_type=F32)
            logits.append(s * ATTN_SCALE + bias_ref[r - rs])
            qss.append(qs)
            kss.append(ks)
        probs, dens = [], []
        for s in logits:
            e = jnp.exp(s - jnp.max(s, axis=-1, keepdims=True))
            dens.append(jnp.sum(e, axis=-1, keepdims=True))
            probs.append(e.astype(BF16))
        for qs, ks, p, den in zip(qss, kss, probs, dens):
            o = jnp.dot(p, v_ref[ks, :], preferred_element_type=F32) / den
            o_ref[qs, :] = o.astype(o_ref.dtype)
        return carry

    lax.fori_loop(0, NA_CHUNK_ROWS // NA_GROUP, group, 0)


def _neighbourhood_attention(qkv, bias):
    tq = NA_CHUNK_ROWS * GRID_W
    return pl.pallas_call(
        _na_body,
        grid=(NA_HEADS, ROWS // NA_CHUNK_ROWS),
        in_specs=[
            pl.BlockSpec((tq, HEAD_DIM), lambda h, c: (c, NA_Q_BLK + h)),
            pl.BlockSpec((SEQ, HEAD_DIM), lambda h, c: (0, NA_K_BLK + h)),
            pl.BlockSpec((SEQ, HEAD_DIM), lambda h, c: (0, NA_V_BLK + h)),
            pl.BlockSpec((NA_ROWS, None, GRID_W, NA_KEYS), lambda h, c: (0, h, 0, 0)),
        ],
        out_specs=pl.BlockSpec((tq, HEAD_DIM), lambda h, c: (c, h)),
        out_shape=jax.ShapeDtypeStruct((SEQ, NA_HEADS * HEAD_DIM), BF16),
        compiler_params=_params(32, 2),
        name="na_attn",
    )(qkv, qkv, qkv, bias)


WG_CHUNK_BLOCKS = 8
WG_PAIR = 2
WG_KEYS = 3 * WG_BLOCK


def _wg_body(hs_ref, q_ref, k_ref, v_ref, o_ref):
    kv = pl.program_id(0)
    chunk = pl.program_id(1)

    heads = [(slice(g * HEAD_DIM, (g + 1) * HEAD_DIM), hs_ref[0, kv * WG_GROUP + g], hs_ref[1, kv * WG_GROUP + g])
             for g in range(WG_GROUP)]

    def blocks(jj, carry):
        work, logits = [], []
        for t in range(WG_PAIR):
            j = jj * WG_PAIR + t
            n = chunk * WG_CHUNK_BLOCKS + j
            start = jnp.clip((n - 1) * WG_BLOCK, 0, SEQ - WG_KEYS)
            ks = pl.ds(pl.multiple_of(start, WG_BLOCK), WG_KEYS)
            qs = pl.ds(pl.multiple_of(j * WG_BLOCK, WG_BLOCK), WG_BLOCK)
            kw = k_ref[ks, :]
            q_pos = n * WG_BLOCK + lax.broadcasted_iota(jnp.int32, (WG_BLOCK, WG_KEYS), 0)
            k_pos = start + lax.broadcasted_iota(jnp.int32, (WG_BLOCK, WG_KEYS), 1)
            dist = jnp.abs(k_pos - q_pos)
            valid = dist <= WG_WINDOW
            dist_f = dist.astype(F32)
            for cols, sink, slope in heads:
                s = lax.dot_general(q_ref[qs, cols], kw, (((1,), (1,)), ((), ())), preferred_element_type=F32)
                logits.append(jnp.where(valid, s * ATTN_SCALE - slope * dist_f, MASKED))
                work.append((qs, ks, cols, sink))
        probs, dens = [], []
        for s, (qs, ks, cols, sink) in zip(logits, work):
            m = jnp.maximum(jnp.max(s, axis=-1, keepdims=True), sink)
            e = jnp.exp(s - m)
            dens.append(jnp.sum(e, axis=-1, keepdims=True) + jnp.exp(sink - m))
            probs.append(e.astype(BF16))
        for p, den, (qs, ks, cols, sink) in zip(probs, dens, work):
            o = jnp.dot(p, v_ref[ks, :], preferred_element_type=F32) / den
            o_ref[qs, cols] = o.astype(o_ref.dtype)
        return carry

    lax.fori_loop(0, WG_CHUNK_BLOCKS // WG_PAIR, blocks, 0)


def _windowed_gqa(qkv, head_scalars):
    tq = WG_CHUNK_BLOCKS * WG_BLOCK
    gw = WG_GROUP * HEAD_DIM
    return pl.pallas_call(
        _wg_body,
        grid=(WG_KV_HEADS, SEQ // tq),
        in_specs=[
            pl.BlockSpec(memory_space=pltpu.SMEM),
            pl.BlockSpec((tq, gw), lambda k, c: (c, WG_Q_BLK + k)),
            pl.BlockSpec((SEQ, HEAD_DIM), lambda k, c: (0, WG_K_BLK + k)),
            pl.BlockSpec((SEQ, HEAD_DIM), lambda k, c: (0, WG_V_BLK + k)),
        ],
        out_specs=pl.BlockSpec((tq, gw), lambda k, c: (c, k)),
        out_shape=jax.ShapeDtypeStruct((SEQ, WG_HEADS * HEAD_DIM), BF16),
        compiler_params=_params(32, 2),
        name="wg_attn",
    )(head_scalars, qkv, qkv, qkv)


CONV_TM = 512
HALO = 8


def _conv_body(bg_ref, cg_ref, hc_ref, cgp_ref, hcp_ref, cgn_ref, hcn_ref, w_ref, o_ref):
    i = pl.program_id(0)
    u = cg_ref[...] * hc_ref[...]
    prev_row = cgp_ref[HALO - 1:HALO, :] * hcp_ref[HALO - 1:HALO, :]
    next_row = cgn_ref[0:1, :] * hcn_ref[0:1, :]
    prev_row = jnp.where(i == 0, 0.0, prev_row)
    next_row = jnp.where(i == pl.num_programs(0) - 1, 0.0, next_row)
    row = lax.broadcasted_iota(jnp.int32, u.shape, 0)
    u_prev = jnp.where(row == 0, prev_row, pltpu.roll(u, 1, 0))
    u_next = jnp.where(row == CONV_TM - 1, next_row, pltpu.roll(u, CONV_TM - 1, 0))
    y = bg_ref[...] * (w_ref[0:1, :] * u_prev + w_ref[1:2, :] * u + w_ref[2:3, :] * u_next)
    o_ref[...] = y.astype(o_ref.dtype)


def _short_conv(rest, conv_w):
    nb = CONV_TM // HALO
    last = SEQ // HALO - 1
    main = lambda col: pl.BlockSpec((CONV_TM, SC_WIDTH), lambda i: (i, col))
    prev = lambda col: pl.BlockSpec((HALO, SC_WIDTH), lambda i: (jnp.maximum(i * nb - 1, 0), col))
    nxt = lambda col: pl.BlockSpec((HALO, SC_WIDTH), lambda i: (jnp.minimum((i + 1) * nb, last), col))
    return pl.pallas_call(
        _conv_body,
        grid=(SEQ // CONV_TM,),
        in_specs=[main(0), main(1), main(2), prev(1), prev(2), nxt(1), nxt(2),
                  pl.BlockSpec((3, SC_WIDTH), lambda i: (0, 0))],
        out_specs=pl.BlockSpec((CONV_TM, SC_WIDTH), lambda i: (i, 0)),
        out_shape=jax.ShapeDtypeStruct((SEQ, SC_WIDTH), BF16),
        compiler_params=_params(40, 1),
        name="short_conv",
    )(rest, rest, rest, rest, rest, rest, rest, conv_w)


def _layer_norm(y, g, b):
    mu = jnp.mean(y, axis=-1, keepdims=True)
    yc = y - mu
    var = jnp.mean(yc * yc, axis=-1, keepdims=True)
    return yc * lax.rsqrt(var + LN_EPS) * g + b


MIX_TM = 256
GATE_TN = 1024


def _mix_body(ya_ref, yb_ref, yc_ref, g00, g01, g10, g11, g20, g21, bg_ref, wb_ref, wo_ref, x_ref, g_ref, b_ref,
              wr_ref, xo_ref, res_ref, aff_ref):
    y_refs = (ya_ref, yb_ref, yc_ref)
    gate_refs = ((g00, g01), (g10, g11), (g20, g21))
    halves = []
    for half in range(D_MODEL // GATE_TN):
        acc = jnp.zeros((MIX_TM, GATE_TN), F32)
        for n in range(N_BRANCH):
            col0 = n * D_MODEL + half * GATE_TN
            branch = jnp.dot(y_refs[n][...],
                             wb_ref[n * BRANCH_WIDTH:(n + 1) * BRANCH_WIDTH, half * GATE_TN:(half + 1) * GATE_TN],
                             preferred_element_type=F32)
            gate = jax.nn.sigmoid(gate_refs[n][half][...] + bg_ref[:, col0:col0 + GATE_TN])
            acc = acc + gate * branch
        halves.append(acc.astype(BF16))
    mix = sum(jnp.dot(m, wo_ref[h * GATE_TN:(h + 1) * GATE_TN, :], preferred_element_type=F32)
              for h, m in enumerate(halves))
    x1 = _layer_norm(ALPHA * x_ref[...] + mix, g_ref[...], b_ref[...])
    xo_ref[...] = x1
    res_ref[...] = ALPHA * x1
    logits = lax.dot_general(wr_ref[...], x1.astype(BF16), (((1,), (1,)), ((), ())),
                             preferred_element_type=F32)
    e = jnp.exp(logits - jnp.max(logits, axis=0, keepdims=True))
    aff_ref[...] = e / jnp.sum(e, axis=0, keepdims=True)


def _mix_ln(ya, yb, yc, rest, b_gate, w_branch_b, w_out_b, x, g, b, w_router_t):
    row = lambda n: pl.BlockSpec((MIX_TM, n), lambda i: (i, 0))
    gate = lambda n, half: pl.BlockSpec(
        (MIX_TM, GATE_TN), lambda i: (i, (GATE_COL + n * D_MODEL) // GATE_TN + half))
    full = lambda a: pl.BlockSpec(a.shape, lambda i: (0,) * a.ndim)
    once = lambda a: pl.BlockSpec(a.shape, lambda i: (0,) * a.ndim, pipeline_mode=pl.Buffered(1))
    return pl.pallas_call(
        _mix_body,
        grid=(SEQ // MIX_TM,),
        in_specs=[row(BRANCH_WIDTH), row(BRANCH_WIDTH), row(BRANCH_WIDTH)]
                 + [gate(n, half) for n in range(N_BRANCH) for half in range(2)]
                 + [full(b_gate), once(w_branch_b), once(w_out_b), row(D_MODEL), full(g), full(b), full(w_router_t)],
        out_specs=[row(D_MODEL), row(D_MODEL), pl.BlockSpec((N_EXPERTS, MIX_TM), lambda i: (0, i))],
        out_shape=[jax.ShapeDtypeStruct((SEQ, D_MODEL), F32), jax.ShapeDtypeStruct((SEQ, D_MODEL), F32),
                   jax.ShapeDtypeStruct((N_EXPERTS, SEQ), F32)],
        compiler_params=_params(58, 1),
        name="mix_ln",
    )(ya, yb, yc, rest, rest, rest, rest, rest, rest, b_gate, w_branch_b, w_out_b, x, g, b, w_router_t)


FINAL_TM = 512


def _final_body(y_ref, g_ref, b_ref, xo_ref, xb_ref):
    x2 = _layer_norm(y_ref[...], g_ref[...], b_ref[...])
    xo_ref[...] = x2
    xb_ref[...] = x2.astype(BF16)


def _final_ln(y, g, b):
    row = pl.BlockSpec((FINAL_TM, D_MODEL), lambda i: (i, 0))
    vec = pl.BlockSpec((1, D_MODEL), lambda i: (0, 0))
    return pl.pallas_call(
        _final_body,
        grid=(SEQ // FINAL_TM,),
        in_specs=[row, vec, vec],
        out_specs=[row, row],
        out_shape=[jax.ShapeDtypeStruct((SEQ, D_MODEL), F32), jax.ShapeDtypeStruct((SEQ, D_MODEL), BF16)],
        compiler_params=_params(48, 1),
        name="final_ln",
    )(y, g, b)


FFN_TF = 256
FFN_STEPS = D_FF // FFN_TF
GATHER_STEPS = 4
GATHER_ROWS = CAPACITY // GATHER_STEPS
GATHER_FIRST = FFN_STEPS - GATHER_STEPS
XROWS_PER_STEP = 168
XROWS_TAIL = CAPACITY - FFN_STEPS * XROWS_PER_STEP
DOWN_HALF = D_MODEL // 2


def _row_copy(x_hbm, rows_ref, sem, src_row, dst_row):
    return pltpu.make_async_copy(x_hbm.at[pl.ds(src_row, 1), :], rows_ref.at[pl.ds(dst_row, 1), :], sem)


ROW_GROUP = 8


def _start_rows(x_hbm, rows_ref, sem, idx_ref, first, count):
    def issue(g, c):
        base = pl.multiple_of(first + g * ROW_GROUP, ROW_GROUP)
        for t in range(ROW_GROUP):
            _row_copy(x_hbm, rows_ref, sem, idx_ref[0, base + t], base + t).start()
        return c
    lax.fori_loop(0, count // ROW_GROUP, issue, 0)


def _wait_rows(x_hbm, rows_ref, sem, count):
    def wait(j, c):
        _row_copy(x_hbm, rows_ref, sem, 0, 0).wait()
        return c
    lax.fori_loop(0, count, wait, 0, unroll=8)


def _ffn_body(l_ref, idx_ref, idx_next_ref, x_hbm, res_hbm, gate_ref, wg_ref, wu_ref, wd_ref, acc_hbm,
              rows_ref, xe_ref, ffn_ref, accb_ref, sem_x, sem_g, sem_s):
    del res_hbm
    e = pl.program_id(0)
    f = pl.program_id(1)
    last = f == FFN_STEPS - 1

    @pl.when((e == 0) & (f == 0))
    def _():
        _start_rows(x_hbm, rows_ref, sem_x, idx_ref, 0, CAPACITY)

    @pl.when(f == 0)
    def _():
        _wait_rows(x_hbm, rows_ref, sem_x, CAPACITY)
        _cast_into(rows_ref, xe_ref, CAPACITY)
        _start_rows(x_hbm, rows_ref, sem_x, idx_next_ref, FFN_STEPS * XROWS_PER_STEP, XROWS_TAIL)

        def zero(i, c):
            sl = pl.ds(pl.multiple_of(i * CAST_ROWS, CAST_ROWS), CAST_ROWS)
            ffn_ref[sl, :] = jnp.zeros((CAST_ROWS, D_MODEL), F32)
            return c
        lax.fori_loop(0, CAPACITY // CAST_ROWS, zero, 0)

    @pl.when((e > 0) & (f == GATHER_FIRST))
    def _():
        _wait_rows(accb_ref, acc_hbm, sem_s, CAPACITY)

    def step(k):
        for t in range(XROWS_PER_STEP):
            j = k * XROWS_PER_STEP + t
            _row_copy(x_hbm, rows_ref, sem_x, idx_next_ref[0, j], j).start()
        if k >= GATHER_FIRST:
            for t in range(GATHER_ROWS):
                j = (k - GATHER_FIRST) * GATHER_ROWS + t
                _row_copy(acc_hbm, accb_ref, sem_g, idx_ref[0, j], j).start()
        xe = xe_ref[...]
        a = jnp.dot(xe, wg_ref[...].astype(BF16), preferred_element_type=F32)
        u = jnp.dot(xe, wu_ref[...].astype(BF16), preferred_element_type=F32)
        hid = (jax.nn.silu(a) * u).astype(BF16)
        for half in range(2):
            cols = slice(half * DOWN_HALF, (half + 1) * DOWN_HALF)
            ffn_ref[:, cols] += jnp.dot(hid, wd_ref[:, cols].astype(BF16), preferred_element_type=F32)

    for k in range(FFN_STEPS):
        pl.when(f == k)(functools.partial(step, k))

    @pl.when(last)
    def _():
        _wait_rows(acc_hbm, accb_ref, sem_g, CAPACITY)

        def update(i, c):
            sl = pl.ds(pl.multiple_of(i * CAST_ROWS, CAST_ROWS), CAST_ROWS)
            accb_ref[sl, :] += ffn_ref[sl, :] * gate_ref[sl, :]
            return c
        lax.fori_loop(0, CAPACITY // CAST_ROWS, update, 0)

        def issue(g, c):
            base = pl.multiple_of(g * ROW_GROUP, ROW_GROUP)
            for t in range(ROW_GROUP):
                _row_copy(accb_ref, acc_hbm, sem_s, base + t, idx_ref[0, base + t]).start()
            return c
        lax.fori_loop(0, CAPACITY // ROW_GROUP, issue, 0)

    @pl.when(last & (e == N_EXPERTS - 1))
    def _():
        _wait_rows(accb_ref, acc_hbm, sem_s, CAPACITY)
        _wait_rows(x_hbm, rows_ref, sem_x, CAPACITY)


def _expert_ffn_accumulate(l, idx, x1, res, gates, w_gate, w_up, w_down):
    cur = lambda e, f, l_ref: (e, 0, 0)
    nxt = lambda e, f, l_ref: (jnp.minimum(e + 1, N_EXPERTS - 1), 0, 0)
    rows = lambda dt: pltpu.VMEM((CAPACITY, D_MODEL), dt)
    return pl.pallas_call(
        _ffn_body,
        grid_spec=pltpu.PrefetchScalarGridSpec(
            num_scalar_prefetch=1,
            grid=(N_EXPERTS, FFN_STEPS),
            in_specs=[
                pl.BlockSpec((None, 1, CAPACITY), cur, memory_space=pltpu.SMEM),
                pl.BlockSpec((None, 1, CAPACITY), nxt, memory_space=pltpu.SMEM),
                pl.BlockSpec(memory_space=pl.ANY),
                pl.BlockSpec(memory_space=pl.ANY),
                pl.BlockSpec((None, CAPACITY, 1), cur),
                pl.BlockSpec((None, None, D_MODEL, FFN_TF), lambda e, f, l_ref: (l_ref[0], e, 0, f)),
                pl.BlockSpec((None, None, D_MODEL, FFN_TF), lambda e, f, l_ref: (l_ref[0], e, 0, f)),
                pl.BlockSpec((None, None, FFN_TF, D_MODEL), lambda e, f, l_ref: (l_ref[0], e, f, 0)),
            ],
            out_specs=pl.BlockSpec(memory_space=pl.ANY),
            scratch_shapes=[rows(F32), rows(BF16), rows(F32), rows(F32),
                            pltpu.SemaphoreType.DMA, pltpu.SemaphoreType.DMA, pltpu.SemaphoreType.DMA],
        ),
        out_shape=jax.ShapeDtypeStruct((SEQ, D_MODEL), F32),
        input_output_aliases={4: 0},
        compiler_params=_params(58, 2),
        name="expert_ffn",
    )(l, idx, idx, x1, res, gates, w_gate, w_up, w_down)


def kernel(x, w_in, b_gate, rpb, sink, conv_w, w_branch, w_out, ln_g, ln_b, w_router, w_gate, w_up, w_down):
    x0 = x.reshape(SEQ, D_MODEL)
    na_bias = _na_bias_tables(rpb)
    slopes = 2.0 ** (-8.0 * jnp.arange(1, WG_HEADS + 1, dtype=F32) / WG_HEADS)
    head_scalars = jnp.stack([sink, jnp.broadcast_to(slopes, sink.shape)], axis=1)
    w_router_t = jnp.swapaxes(w_router, 1, 2).astype(BF16)

    def layer(carry, per_layer):
        xf, xb = carry
        li, bias_l, hs_l, conv_l, bg_l, g_l, b_l, wr_l = per_layer
        l = li.reshape(1)
        qkv = _project(l, xb, w_in, 0, QKV_WIDTH, BF16)
        rest = _project(l, xb, w_in, QKV_WIDTH, REST_WIDTH, F32)
        ya = _neighbourhood_attention(qkv, bias_l)
        yb = _windowed_gqa(qkv, hs_l)
        yc = _short_conv(rest, conv_l)
        w_branch_b = _cast_branch(l, w_branch, 512)
        w_out_b = _cast_layer(l, w_out, 512)
        x1, res, aff_t = _mix_ln(ya, yb, yc, rest, bg_l.reshape(1, N_BRANCH * D_MODEL), w_branch_b, w_out_b,
                                 xf, g_l[0:1], b_l[0:1], wr_l)
        gates, idx = lax.top_k(aff_t, CAPACITY)
        y = _expert_ffn_accumulate(l, idx[:, None, :], x1, res, gates[..., None], w_gate, w_up, w_down)
        x2, x2b = _final_ln(y, g_l[1:2], b_l[1:2])
        return (x2, x2b), None

    per_layer = (jnp.arange(DEPTH, dtype=jnp.int32), na_bias, head_scalars, conv_w, b_gate, ln_g, ln_b, w_router_t)
    (xf, _), _ = lax.scan(layer, (x0, _cast_rows(x0, 512)), per_layer)
    return xf.reshape(x.shape)
```

```python
import functools

import jax
import jax.numpy as jnp
from jax import lax
from jax.experimental import pallas as pl
from jax.experimental.pallas import tpu as pltpu

F32 = jnp.float32
BF16 = jnp.bfloat16

D_MODEL = 2048
SEQ = 8192
DEPTH = 4
GRID_W = 64
ROWS = SEQ // GRID_W
HEAD_DIM = 128
NA_HEADS = 8
NA_ROWS = 8
NA_COLS = 16
WG_HEADS = 8
WG_KV_HEADS = 2
WG_GROUP = WG_HEADS // WG_KV_HEADS
WG_WINDOW = 128
WG_BLOCK = 128
SC_WIDTH = 1024
BRANCH_WIDTH = 1024
N_BRANCH = 3
N_EXPERTS = 16
CAPACITY = 2 * SEQ // N_EXPERTS
D_FF = 1536
ALPHA = (2 * DEPTH) ** 0.25
LN_EPS = 1e-5
ATTN_SCALE = HEAD_DIM ** -0.5
MASKED = -1e30

QKV_WIDTH = 3 * NA_HEADS * HEAD_DIM + (WG_HEADS + 2 * WG_KV_HEADS) * HEAD_DIM
REST_WIDTH = 3 * SC_WIDTH + N_BRANCH * D_MODEL
NA_Q_BLK, NA_K_BLK, NA_V_BLK = 0, NA_HEADS, 2 * NA_HEADS
WG_Q_BLK = 3 * NA_HEADS * HEAD_DIM // (WG_GROUP * HEAD_DIM)
WG_K_BLK = (3 * NA_HEADS + WG_HEADS) * HEAD_DIM // HEAD_DIM
WG_V_BLK = WG_K_BLK + WG_KV_HEADS
GATE_COL = 3 * SC_WIDTH

MIB = 1024 * 1024


def _params(vmem_mib, n_axes):
    return pltpu.CompilerParams(dimension_semantics=("arbitrary",) * n_axes,
                                vmem_limit_bytes=vmem_mib * MIB)


def _cast_body(x_ref, o_ref):
    o_ref[...] = x_ref[...].astype(BF16)


def _cast_rows(x, tm):
    m, n = x.shape
    return pl.pallas_call(
        _cast_body,
        grid=(m // tm,),
        in_specs=[pl.BlockSpec((tm, n), lambda i: (i, 0))],
        out_specs=pl.BlockSpec((tm, n), lambda i: (i, 0)),
        out_shape=jax.ShapeDtypeStruct((m, n), BF16),
        compiler_params=_params(32, 1),
        name="cast_rows",
    )(x)


def _cast_layer_body(l_ref, x_ref, o_ref):
    o_ref[...] = x_ref[...].astype(BF16)


def _cast_layer(l, w, tm):
    _, m, n = w.shape
    return pl.pallas_call(
        _cast_layer_body,
        grid_spec=pltpu.PrefetchScalarGridSpec(
            num_scalar_prefetch=1,
            grid=(m // tm,),
            in_specs=[pl.BlockSpec((None, tm, n), lambda i, l_ref: (l_ref[0], i, 0))],
            out_specs=pl.BlockSpec((tm, n), lambda i, l_ref: (i, 0)),
        ),
        out_shape=jax.ShapeDtypeStruct((m, n), BF16),
        compiler_params=_params(32, 1),
        name="cast_layer",
    )(l, w)


def _cast_branch(l, w, tm):
    _, nb, m, n = w.shape
    per = m // tm
    return pl.pallas_call(
        _cast_layer_body,
        grid_spec=pltpu.PrefetchScalarGridSpec(
            num_scalar_prefetch=1,
            grid=(nb, per),
            in_specs=[pl.BlockSpec((None, None, tm, n), lambda b, i, l_ref: (l_ref[0], b, i, 0))],
            out_specs=pl.BlockSpec((tm, n), lambda b, i, l_ref: (b * per + i, 0)),
        ),
        out_shape=jax.ShapeDtypeStruct((nb * m, n), BF16),
        compiler_params=_params(32, 2),
        name="cast_branch",
    )(l, w)


PROJ_TM = 1024
PROJ_TN = 1536
CAST_ROWS = 256


def _cast_into(w_ref, wb_ref, rows):
    def chunk(i, c):
        sl = pl.ds(pl.multiple_of(i * CAST_ROWS, CAST_ROWS), CAST_ROWS)
        wb_ref[sl, :] = w_ref[sl, :].astype(BF16)
        return c
    lax.fori_loop(0, rows // CAST_ROWS, chunk, 0)


def _proj_body(l_ref, x_ref, w_ref, o_ref, wb_ref):
    @pl.when(pl.program_id(1) == 0)
    def _():
        _cast_into(w_ref, wb_ref, D_MODEL)

    o_ref[...] = jnp.dot(x_ref[...], wb_ref[...], preferred_element_type=F32).astype(o_ref.dtype)


def _project(l, xb, w_in, col0, width, out_dtype):
    n_tiles, off = width // PROJ_TN, col0 // PROJ_TN
    return pl.pallas_call(
        _proj_body,
        grid_spec=pltpu.PrefetchScalarGridSpec(
            num_scalar_prefetch=1,
            grid=(n_tiles, SEQ // PROJ_TM),
            in_specs=[
                pl.BlockSpec((PROJ_TM, D_MODEL), lambda j, i, l_ref: (i, 0)),
                pl.BlockSpec((None, D_MODEL, PROJ_TN), lambda j, i, l_ref: (l_ref[0], 0, off + j)),
            ],
            out_specs=pl.BlockSpec((PROJ_TM, PROJ_TN), lambda j, i, l_ref: (i, j)),
            scratch_shapes=[pltpu.VMEM((D_MODEL, PROJ_TN), BF16)],
        ),
        out_shape=jax.ShapeDtypeStruct((SEQ, width), out_dtype),
        compiler_params=_params(58, 2),
        name="in_proj",
    )(l, xb, w_in)


NA_CHUNK_ROWS = 32
NA_GROUP = 16
NA_KEYS = NA_ROWS * GRID_W


def _na_bias_tables(rpb):
    c = jnp.arange(GRID_W)[:, None]
    cj = jnp.arange(GRID_W)[None, :]
    cs = jnp.clip(c - NA_COLS // 2, 0, GRID_W - NA_COLS)
    valid = (cj >= cs) & (cj < cs + NA_COLS)
    select = ((cj - c + (NA_COLS - 1))[None] == jnp.arange(2 * NA_COLS - 1)[:, None, None]) & valid[None]
    t = jnp.einsum('lhrk,kcj->lhrcj', rpb.astype(F32), select.astype(F32),
                   precision=lax.Precision.HIGHEST)
    d = jnp.arange(NA_ROWS)[:, None, None]
    i = jnp.arange(NA_ROWS)[None, :, None]
    row_select = (i - d + (NA_ROWS - 1)) == jnp.arange(2 * NA_ROWS - 1)[None, None, :]
    t = jnp.einsum('dir,lhrcj->ldhcij', row_select.astype(F32), t, precision=lax.Precision.HIGHEST)
    t = jnp.where(valid[None, None, None, :, None, :], t, MASKED)
    return t.reshape(DEPTH, NA_ROWS, NA_HEADS, GRID_W, NA_KEYS)


def _na_body(q_ref, k_ref, v_ref, bias_ref, o_ref):
    chunk = pl.program_id(1)

    def group(gi, carry):
        qss, kss, logits = [], [], []
        for t in range(NA_GROUP):
            i = gi * NA_GROUP + t
            r = chunk * NA_CHUNK_ROWS + i
            rs = jnp.clip(r - NA_ROWS // 2, 0, ROWS - NA_ROWS)
            qs = pl.ds(pl.multiple_of(i * GRID_W, GRID_W), GRID_W)
            ks = pl.ds(pl.multiple_of(rs * GRID_W, GRID_W), NA_KEYS)
            s = lax.dot_general(q_ref[qs, :], k_ref[ks, :], (((1,), (1,)), ((), ())),
                                preferred_element_type=F32)
            logits.append(s * ATTN_SCALE + bias_ref[r - rs])
            qss.append(qs)
            kss.append(ks)
        probs, dens = [], []
        for s in logits:
            e = jnp.exp(s - jnp.max(s, axis=-1, keepdims=True))
            dens.append(jnp.sum(e, axis=-1, keepdims=True))
            probs.append(e.astype(BF16))
        for qs, ks, p, den in zip(qss, kss, probs, dens):
            o = jnp.dot(p, v_ref[ks, :], preferred_element_type=F32) / den
            o_ref[qs, :] = o.astype(o_ref.dtype)
        return carry

    lax.fori_loop(0, NA_CHUNK_ROWS // NA_GROUP, group, 0)


def _neighbourhood_attention(qkv, bias):
    tq = NA_CHUNK_ROWS * GRID_W
    return pl.pallas_call(
        _na_body,
        grid=(NA_HEADS, ROWS // NA_CHUNK_ROWS),
        in_specs=[
            pl.BlockSpec((tq, HEAD_DIM), lambda h, c: (c, NA_Q_BLK + h)),
            pl.BlockSpec((SEQ, HEAD_DIM), lambda h, c: (0, NA_K_BLK + h)),
            pl.BlockSpec((SEQ, HEAD_DIM), lambda h, c: (0, NA_V_BLK + h)),
            pl.BlockSpec((NA_ROWS, None, GRID_W, NA_KEYS), lambda h, c: (0, h, 0, 0)),
        ],
        out_specs=pl.BlockSpec((tq, HEAD_DIM), lambda h, c: (c, h)),
        out_shape=jax.ShapeDtypeStruct((SEQ, NA_HEADS * HEAD_DIM), BF16),
        compiler_params=_params(32, 2),
        name="na_attn",
    )(qkv, qkv, qkv, bias)


WG_CHUNK_BLOCKS = 8
WG_PAIR = 2
WG_OFFSETS = 3
WG_KEYS = 3 * WG_BLOCK


def _wg_body(hs_ref, q_ref, k_ref, v_ref, o_ref, tab_ref):
    kv = pl.program_id(0)
    chunk = pl.program_id(1)

    heads = [(slice(g * HEAD_DIM, (g + 1) * HEAD_DIM), hs_ref[0, kv * WG_GROUP + g], hs_ref[1, kv * WG_GROUP + g])
             for g in range(WG_GROUP)]
    for v in range(WG_OFFSETS):
        rel = (v - (WG_OFFSETS - 1)) * WG_BLOCK + (lax.broadcasted_iota(jnp.int32, (WG_BLOCK, WG_KEYS), 1)
                                                   - lax.broadcasted_iota(jnp.int32, (WG_BLOCK, WG_KEYS), 0))
        dist = jnp.abs(rel)
        for g, (cols, sink, slope) in enumerate(heads):
            tab_ref[v, g] = jnp.where(dist <= WG_WINDOW, -slope * dist.astype(F32), MASKED)

    def blocks(jj, carry):
        work, logits = [], []
        for t in range(WG_PAIR):
            j = jj * WG_PAIR + t
            n = chunk * WG_CHUNK_BLOCKS + j
            start = jnp.clip((n - 1) * WG_BLOCK, 0, SEQ - WG_KEYS)
            ks = pl.ds(pl.multiple_of(start, WG_BLOCK), WG_KEYS)
            qs = pl.ds(pl.multiple_of(j * WG_BLOCK, WG_BLOCK), WG_BLOCK)
            kw = k_ref[ks, :]
            offset = (start - n * WG_BLOCK) // WG_BLOCK + (WG_OFFSETS - 1)
            for g, (cols, sink, slope) in enumerate(heads):
                s = lax.dot_general(q_ref[qs, cols], kw, (((1,), (1,)), ((), ())), preferred_element_type=F32)
                logits.append(s * ATTN_SCALE + tab_ref[offset, g])
                work.append((qs, ks, cols, sink))
        probs, dens = [], []
        for s, (qs, ks, cols, sink) in zip(logits, work):
            m = jnp.maximum(jnp.max(s, axis=-1, keepdims=True), sink)
            e = jnp.exp(s - m)
            dens.append(jnp.sum(e, axis=-1, keepdims=True) + jnp.exp(sink - m))
            probs.append(e.astype(BF16))
        for p, den, (qs, ks, cols, sink) in zip(probs, dens, work):
            o = jnp.dot(p, v_ref[ks, :], preferred_element_type=F32) / den
            o_ref[qs, cols] = o.astype(o_ref.dtype)
        return carry

    lax.fori_loop(0, WG_CHUNK_BLOCKS // WG_PAIR, blocks, 0)


def _windowed_gqa(qkv, head_scalars):
    tq = WG_CHUNK_BLOCKS * WG_BLOCK
    gw = WG_GROUP * HEAD_DIM
    return pl.pallas_call(
        _wg_body,
        grid=(WG_KV_HEADS, SEQ // tq),
        in_specs=[
            pl.BlockSpec(memory_space=pltpu.SMEM),
            pl.BlockSpec((tq, gw), lambda k, c: (c, WG_Q_BLK + k)),
            pl.BlockSpec((SEQ, HEAD_DIM), lambda k, c: (0, WG_K_BLK + k)),
            pl.BlockSpec((SEQ, HEAD_DIM), lambda k, c: (0, WG_V_BLK + k)),
        ],
        out_specs=pl.BlockSpec((tq, gw), lambda k, c: (c, k)),
        out_shape=jax.ShapeDtypeStruct((SEQ, WG_HEADS * HEAD_DIM), BF16),
        scratch_shapes=[pltpu.VMEM((WG_OFFSETS, WG_GROUP, WG_BLOCK, WG_KEYS), F32)],
        compiler_params=_params(32, 2),
        name="wg_attn",
    )(head_scalars, qkv, qkv, qkv)


CONV_TM = 512
HALO = 8


def _conv_body(bg_ref, cg_ref, hc_ref, cgp_ref, hcp_ref, cgn_ref, hcn_ref, w_ref, o_ref):
    i = pl.program_id(0)
    u = cg_ref[...] * hc_ref[...]
    prev_row = cgp_ref[HALO - 1:HALO, :] * hcp_ref[HALO - 1:HALO, :]
    next_row = cgn_ref[0:1, :] * hcn_ref[0:1, :]
    prev_row = jnp.where(i == 0, 0.0, prev_row)
    next_row = jnp.where(i == pl.num_programs(0) - 1, 0.0, next_row)
    row = lax.broadcasted_iota(jnp.int32, u.shape, 0)
    u_prev = jnp.where(row == 0, prev_row, pltpu.roll(u, 1, 0))
    u_next = jnp.where(row == CONV_TM - 1, next_row, pltpu.roll(u, CONV_TM - 1, 0))
    y = bg_ref[...] * (w_ref[0:1, :] * u_prev + w_ref[1:2, :] * u + w_ref[2:3, :] * u_next)
    o_ref[...] = y.astype(o_ref.dtype)


def _short_conv(rest, conv_w):
    nb = CONV_TM // HALO
    last = SEQ // HALO - 1
    main = lambda col: pl.BlockSpec((CONV_TM, SC_WIDTH), lambda i: (i, col))
    prev = lambda col: pl.BlockSpec((HALO, SC_WIDTH), lambda i: (jnp.maximum(i * nb - 1, 0), col))
    nxt = lambda col: pl.BlockSpec((HALO, SC_WIDTH), lambda i: (jnp.minimum((i + 1) * nb, last), col))
    return pl.pallas_call(
        _conv_body,
        grid=(SEQ // CONV_TM,),
        in_specs=[main(0), main(1), main(2), prev(1), prev(2), nxt(1), nxt(2),
                  pl.BlockSpec((3, SC_WIDTH), lambda i: (0, 0))],
        out_specs=pl.BlockSpec((CONV_TM, SC_WIDTH), lambda i: (i, 0)),
        out_shape=jax.ShapeDtypeStruct((SEQ, SC_WIDTH), BF16),
        compiler_params=_params(40, 1),
        name="short_conv",
    )(rest, rest, rest, rest, rest, rest, rest, conv_w)


def _layer_norm(y, g, b):
    mu = jnp.mean(y, axis=-1, keepdims=True)
    yc = y - mu
    var = jnp.mean(yc * yc, axis=-1, keepdims=True)
    return yc * lax.rsqrt(var + LN_EPS) * g + b


MIX_TM = 256
GATE_TN = 1024


def _mix_body(ya_ref, yb_ref, yc_ref, g00, g01, g10, g11, g20, g21, bg_ref, wb_ref, wo_ref, x_ref, g_ref, b_ref,
              wr_ref, xo_ref, res_ref, aff_ref):
    y_refs = (ya_ref, yb_ref, yc_ref)
    gate_refs = ((g00, g01), (g10, g11), (g20, g21))
    halves = []
    for half in range(D_MODEL // GATE_TN):
        acc = jnp.zeros((MIX_TM, GATE_TN), F32)
        for n in range(N_BRANCH):
            col0 = n * D_MODEL + half * GATE_TN
            branch = jnp.dot(y_refs[n][...],
                             wb_ref[n * BRANCH_WIDTH:(n + 1) * BRANCH_WIDTH, half * GATE_TN:(half + 1) * GATE_TN],
                             preferred_element_type=F32)
            gate = jax.nn.sigmoid(gate_refs[n][half][...] + bg_ref[:, col0:col0 + GATE_TN])
            acc = acc + gate * branch
        halves.append(acc.astype(BF16))
    mix = sum(jnp.dot(m, wo_ref[h * GATE_TN:(h + 1) * GATE_TN, :], preferred_element_type=F32)
              for h, m in enumerate(halves))
    x1 = _layer_norm(ALPHA * x_ref[...] + mix, g_ref[...], b_ref[...])
    xo_ref[...] = x1
    res_ref[...] = ALPHA * x1
    logits = lax.dot_general(wr_ref[...], x1.astype(BF16), (((1,), (1,)), ((), ())),
                             preferred_element_type=F32)
    e = jnp.exp(logits - jnp.max(logits, axis=0, keepdims=True))
    aff_ref[...] = e / jnp.sum(e, axis=0, keepdims=True)


def _mix_ln(ya, yb, yc, rest, b_gate, w_branch_b, w_out_b, x, g, b, w_router_t):
    row = lambda n: pl.BlockSpec((MIX_TM, n), lambda i: (i, 0))
    gate = lambda n, half: pl.BlockSpec(
        (MIX_TM, GATE_TN), lambda i: (i, (GATE_COL + n * D_MODEL) // GATE_TN + half))
    full = lambda a: pl.BlockSpec(a.shape, lambda i: (0,) * a.ndim)
    once = lambda a: pl.BlockSpec(a.shape, lambda i: (0,) * a.ndim, pipeline_mode=pl.Buffered(1))
    return pl.pallas_call(
        _mix_body,
        grid=(SEQ // MIX_TM,),
        in_specs=[row(BRANCH_WIDTH), row(BRANCH_WIDTH), row(BRANCH_WIDTH)]
                 + [gate(n, half) for n in range(N_BRANCH) for half in range(2)]
                 + [full(b_gate), once(w_branch_b), once(w_out_b), row(D_MODEL), full(g), full(b), full(w_router_t)],
        out_specs=[row(D_MODEL), row(D_MODEL), pl.BlockSpec((N_EXPERTS, MIX_TM), lambda i: (0, i))],
        out_shape=[jax.ShapeDtypeStruct((SEQ, D_MODEL), F32), jax.ShapeDtypeStruct((SEQ, D_MODEL), F32),
                   jax.ShapeDtypeStruct((N_EXPERTS, SEQ), F32)],
        compiler_params=_params(58, 1),
        name="mix_ln",
    )(ya, yb, yc, rest, rest, rest, rest, rest, rest, b_gate, w_branch_b, w_out_b, x, g, b, w_router_t)


FINAL_TM = 512


def _final_body(y_ref, g_ref, b_ref, xo_ref, xb_ref):
    x2 = _layer_norm(y_ref[...], g_ref[...], b_ref[...])
    xo_ref[...] = x2
    xb_ref[...] = x2.astype(BF16)


def _final_ln(y, g, b):
    row = pl.BlockSpec((FINAL_TM, D_MODEL), lambda i: (i, 0))
    vec = pl.BlockSpec((1, D_MODEL), lambda i: (0, 0))
    return pl.pallas_call(
        _final_body,
        grid=(SEQ // FINAL_TM,),
        in_specs=[row, vec, vec],
        out_specs=[row, row],
        out_shape=[jax.ShapeDtypeStruct((SEQ, D_MODEL), F32), jax.ShapeDtypeStruct((SEQ, D_MODEL), BF16)],
        compiler_params=_params(48, 1),
        name="final_ln",
    )(y, g, b)


FFN_TF = 256
FFN_STEPS = D_FF // FFN_TF
GATHER_STEPS = 4
GATHER_ROWS = CAPACITY // GATHER_STEPS
GATHER_FIRST = FFN_STEPS - GATHER_STEPS
XROWS_PER_STEP = 168
XROWS_TAIL = CAPACITY - FFN_STEPS * XROWS_PER_STEP
DOWN_HALF = D_MODEL // 2


def _row_copy(x_hbm, rows_ref, sem, src_row, dst_row):
    return pltpu.make_async_copy(x_hbm.at[pl.ds(src_row, 1), :], rows_ref.at[pl.ds(dst_row, 1), :], sem)


ROW_GROUP = 8


def _start_rows(x_hbm, rows_ref, sem, idx_ref, first, count):
    def issue(g, c):
        base = pl.multiple_of(first + g * ROW_GROUP, ROW_GROUP)
        for t in range(ROW_GROUP):
            _row_copy(x_hbm, rows_ref, sem, idx_ref[0, base + t], base + t).start(priority=t % 2)
        return c
    lax.fori_loop(0, count // ROW_GROUP, issue, 0)


def _wait_rows(x_hbm, rows_ref, sem, count):
    def wait(j, c):
        _row_copy(x_hbm, rows_ref, sem, 0, 0).wait()
        return c
    lax.fori_loop(0, count, wait, 0, unroll=8)


def _ffn_body(l_ref, idx_ref, idx_next_ref, x_hbm, res_hbm, gate_ref, wg_ref, wu_ref, wd_ref, acc_hbm,
              rows_ref, xe_ref, ffn_ref, accb_ref, sem_x, sem_g, sem_s):
    del res_hbm
    e = pl.program_id(0)
    f = pl.program_id(1)
    last = f == FFN_STEPS - 1

    @pl.when((e == 0) & (f == 0))
    def _():
        _start_rows(x_hbm, rows_ref, sem_x, idx_ref, 0, CAPACITY)

    @pl.when(f == 0)
    def _():
        _wait_rows(x_hbm, rows_ref, sem_x, CAPACITY)
        _cast_into(rows_ref, xe_ref, CAPACITY)
        _start_rows(x_hbm, rows_ref, sem_x, idx_next_ref, FFN_STEPS * XROWS_PER_STEP, XROWS_TAIL)

        def zero(i, c):
            sl = pl.ds(pl.multiple_of(i * CAST_ROWS, CAST_ROWS), CAST_ROWS)
            ffn_ref[sl, :] = jnp.zeros((CAST_ROWS, D_MODEL), F32)
            return c
        lax.fori_loop(0, CAPACITY // CAST_ROWS, zero, 0)

    @pl.when((e > 0) & (f == GATHER_FIRST))
    def _():
        _wait_rows(accb_ref, acc_hbm, sem_s, CAPACITY)

    def step(k):
        for t in range(XROWS_PER_STEP):
            j = k * XROWS_PER_STEP + t
            _row_copy(x_hbm, rows_ref, sem_x, idx_next_ref[0, j], j).start(priority=t % 2)
        if k >= GATHER_FIRST:
            for t in range(GATHER_ROWS):
                j = (k - GATHER_FIRST) * GATHER_ROWS + t
                _row_copy(acc_hbm, accb_ref, sem_g, idx_ref[0, j], j).start(priority=t % 2)
        xe = xe_ref[...]
        a = jnp.dot(xe, wg_ref[...].astype(BF16), preferred_element_type=F32)
        u = jnp.dot(xe, wu_ref[...].astype(BF16), preferred_element_type=F32)
        hid = (jax.nn.silu(a) * u).astype(BF16)
        for half in range(2):
            cols = slice(half * DOWN_HALF, (half + 1) * DOWN_HALF)
            ffn_ref[:, cols] += jnp.dot(hid, wd_ref[:, cols].astype(BF16), preferred_element_type=F32)

    for k in range(FFN_STEPS):
        pl.when(f == k)(functools.partial(step, k))

    @pl.when(last)
    def _():
        _wait_rows(acc_hbm, accb_ref, sem_g, CAPACITY)

        def update(i, c):
            sl = pl.ds(pl.multiple_of(i * CAST_ROWS, CAST_ROWS), CAST_ROWS)
            accb_ref[sl, :] += ffn_ref[sl, :] * gate_ref[sl, :]
            return c
        lax.fori_loop(0, CAPACITY // CAST_ROWS, update, 0)

        def issue(g, c):
            base = pl.multiple_of(g * ROW_GROUP, ROW_GROUP)
            for t in range(ROW_GROUP):
                _row_copy(accb_ref, acc_hbm, sem_s, base + t, idx_ref[0, base + t]).start(priority=t % 2)
            return c
        lax.fori_loop(0, CAPACITY // ROW_GROUP, issue, 0)

    @pl.when(last & (e == N_EXPERTS - 1))
    def _():
        _wait_rows(accb_ref, acc_hbm, sem_s, CAPACITY)
        _wait_rows(x_hbm, rows_ref, sem_x, CAPACITY)


def _expert_ffn_accumulate(l, idx, x1, res, gates, w_gate, w_up, w_down):
    cur = lambda e, f, l_ref: (e, 0, 0)
    nxt = lambda e, f, l_ref: (jnp.minimum(e + 1, N_EXPERTS - 1), 0, 0)
    rows = lambda dt: pltpu.VMEM((CAPACITY, D_MODEL), dt)
    return pl.pallas_call(
        _ffn_body,
        grid_spec=pltpu.PrefetchScalarGridSpec(
            num_scalar_prefetch=1,
            grid=(N_EXPERTS, FFN_STEPS),
            in_specs=[
                pl.BlockSpec((None, 1, CAPACITY), cur, memory_space=pltpu.SMEM),
                pl.BlockSpec((None, 1, CAPACITY), nxt, memory_space=pltpu.SMEM),
                pl.BlockSpec(memory_space=pl.ANY),
                pl.BlockSpec(memory_space=pl.ANY),
                pl.BlockSpec((None, CAPACITY, 1), cur),
                pl.BlockSpec((None, None, D_MODEL, FFN_TF), lambda e, f, l_ref: (l_ref[0], e, 0, f)),
                pl.BlockSpec((None, None, D_MODEL, FFN_TF), lambda e, f, l_ref: (l_ref[0], e, 0, f)),
                pl.BlockSpec((None, None, FFN_TF, D_MODEL), lambda e, f, l_ref: (l_ref[0], e, f, 0)),
            ],
            out_specs=pl.BlockSpec(memory_space=pl.ANY),
            scratch_shapes=[rows(F32), rows(BF16), rows(F32), rows(F32),
                            pltpu.SemaphoreType.DMA, pltpu.SemaphoreType.DMA, pltpu.SemaphoreType.DMA],
        ),
        out_shape=jax.ShapeDtypeStruct((SEQ, D_MODEL), F32),
        input_output_aliases={4: 0},
        compiler_params=_params(58, 2),
        name="expert_ffn",
    )(l, idx, idx, x1, res, gates, w_gate, w_up, w_down)


def kernel(x, w_in, b_gate, rpb, sink, conv_w, w_branch, w_out, ln_g, ln_b, w_router, w_gate, w_up, w_down):
    x0 = x.reshape(SEQ, D_MODEL)
    na_bias = _na_bias_tables(rpb)
    slopes = 2.0 ** (-8.0 * jnp.arange(1, WG_HEADS + 1, dtype=F32) / WG_HEADS)
    head_scalars = jnp.stack([sink, jnp.broadcast_to(slopes, sink.shape)], axis=1)
    w_router_t = jnp.swapaxes(w_router, 1, 2).astype(BF16)

    def layer(carry, per_layer):
        xf, xb = carry
        li, bias_l, hs_l, conv_l, bg_l, g_l, b_l, wr_l = per_layer
        l = li.reshape(1)
        qkv = _project(l, xb, w_in, 0, QKV_WIDTH, BF16)
        rest = _project(l, xb, w_in, QKV_WIDTH, REST_WIDTH, F32)
        ya = _neighbourhood_attention(qkv, bias_l)
        yb = _windowed_gqa(qkv, hs_l)
        yc = _short_conv(rest, conv_l)
        w_branch_b = _cast_branch(l, w_branch, 512)
        w_out_b = _cast_layer(l, w_out, 512)
        x1, res, aff_t = _mix_ln(ya, yb, yc, rest, bg_l.reshape(1, N_BRANCH * D_MODEL), w_branch_b, w_out_b,
                                 xf, g_l[0:1], b_l[0:1], wr_l)
        gates, idx = lax.top_k(aff_t, CAPACITY)
        y = _expert_ffn_accumulate(l, idx[:, None, :], x1, res, gates[..., None], w_gate, w_up, w_down)
        x2, x2b = _final_ln(y, g_l[1:2], b_l[1:2])
        return (x2, x2b), None

    per_layer = (jnp.arange(DEPTH, dtype=jnp.int32), na_bias, head_scalars, conv_w, b_gate, ln_g, ln_b, w_router_t)
    (xf, _), _ = lax.scan(layer, (x0, _cast_rows(x0, 512)), per_layer)
    return xf.reshape(x.shape)
```
